```python
import jax
import jax.numpy as jnp
from jax import lax
import numpy as np

D_MODEL = 2048
BATCH = 2
SEQ = 8192
DEPTH = 4

GRID_W = 64
CTX_LEN = 256
N_MIXERS = 3
HEAD_DIM = 128
N_HEADS = D_MODEL // HEAD_DIM
N_KV_HEADS = N_HEADS // 4
GQA_GROUP = N_HEADS // N_KV_HEADS
WINDOW = 128
BLOCK = 128
ROPE_THETA = 10000.0
ROPE_FREQS = HEAD_DIM // 4
LRU_WIDTH = D_MODEL
LRU_BLOCK_SIZE = 128
LRU_BLOCKS = LRU_WIDTH // LRU_BLOCK_SIZE
LRU_C = 8.0
LRU_CONV = 4
CONF_WIDTH = 31
D_FF = ((8 * D_MODEL // 3 + 255) // 256) * 256
FFN_CONV = 3
NORM_EPS = 1e-6
NEG_INF = -1e30
N_ATTN_LAYERS = (DEPTH + 2) // 3
N_LRU_LAYERS = (DEPTH + 1) // 3
N_CONV_LAYERS = DEPTH // 3

kernel_name = "hybrid_interleaved_diffusion_block"


def rms_norm(x, g):
    x32 = x.astype(jnp.float32)
    y = x32 * lax.rsqrt(jnp.mean(x32 * x32, axis=-1, keepdims=True) + NORM_EPS)
    return y.astype(x.dtype) * g


def layer_norm(x, g, b):
    x32 = x.astype(jnp.float32)
    mu = jnp.mean(x32, axis=-1, keepdims=True)
    var = jnp.mean(jnp.square(x32 - mu), axis=-1, keepdims=True)
    return ((x32 - mu) * lax.rsqrt(var + NORM_EPS)).astype(x.dtype) * g + b


def modulate(x, g, shift, scale):
    return rms_norm(x, g) * (1.0 + scale) + shift


def ada_mod(cvec, w, b):
    m = jax.nn.silu(cvec) @ w + b
    return [t[:, None, :] for t in jnp.split(m, 6, axis=-1)]


def depthwise_conv(x, w, b, pad_left, pad_right):
    y = lax.conv_general_dilated(
        x, w[:, None, :].astype(x.dtype), window_strides=(1,),
        padding=[(pad_left, pad_right)], dimension_numbers=("NWC", "WIO", "NWC"),
        feature_group_count=x.shape[-1])
    return y + b


def axial_rope_tables(n_tokens):
    rows = n_tokens // GRID_W
    row = jnp.repeat(jnp.arange(rows), GRID_W)
    col = jnp.arange(rows * GRID_W) % GRID_W
    pos = jnp.stack([row, col], axis=-1).astype(jnp.float32)
    freq = ROPE_THETA ** (-jnp.arange(ROPE_FREQS, dtype=jnp.float32) / ROPE_FREQS)
    ang = pos[:, :, None] * freq
    return jnp.cos(ang), jnp.sin(ang)


def apply_axial_rope(x, cos, sin):
    b, s, n, _ = x.shape
    xr = x.reshape(b, s, n, 2, 2, ROPE_FREQS)
    x1, x2 = xr[..., 0, :], xr[..., 1, :]
    cs = cos[None, :, None].astype(x.dtype)
    sn = sin[None, :, None].astype(x.dtype)
    return jnp.stack([x1 * cs - x2 * sn, x2 * cs + x1 * sn], axis=-2).reshape(x.shape)


def attention_mixer(h, hc, w_qkv, w_o, sink, cos, sin, need_ctx_out):
    bsz, n_tok, _ = h.shape
    n_ctx = hc.shape[1]
    q_cols = N_HEADS * HEAD_DIM
    kv_cols = N_KV_HEADS * HEAD_DIM
    scale = HEAD_DIM ** -0.5
    qkv = h @ w_qkv
    q = apply_axial_rope(qkv[..., :q_cols].reshape(bsz, n_tok, N_HEADS, HEAD_DIM), cos, sin)
    k = apply_axial_rope(qkv[..., q_cols:q_cols + kv_cols].reshape(bsz, n_tok, N_KV_HEADS, HEAD_DIM), cos, sin)
    v = qkv[..., q_cols + kv_cols:].reshape(bsz, n_tok, N_KV_HEADS, HEAD_DIM)
    kv_c = hc @ w_qkv[:, q_cols:]
    k_c = kv_c[..., :kv_cols].reshape(bsz, n_ctx, N_KV_HEADS, HEAD_DIM)
    v_c = kv_c[..., kv_cols:].reshape(bsz, n_ctx, N_KV_HEADS, HEAD_DIM)
    sink_kg = sink.astype(jnp.float32).reshape(N_KV_HEADS, GQA_GROUP)

    n_blk = n_tok // BLOCK
    qb = q.reshape(bsz, n_blk, BLOCK, N_KV_HEADS, GQA_GROUP, HEAD_DIM)

    def band(t):
        tp = jnp.pad(t, ((0, 0), (BLOCK, BLOCK), (0, 0), (0, 0)))
        tp = tp.reshape(bsz, n_blk + 2, BLOCK, N_KV_HEADS, HEAD_DIM)
        return jnp.concatenate([tp[:, :-2], tp[:, 1:-1], tp[:, 2:]], axis=2)

    kb, vb = band(k), band(v)
    s_loc = jnp.einsum('bnqkgd,bnskd->bnkgqs', qb, kb).astype(jnp.float32) * scale
    s_ctx = jnp.einsum('bnqkgd,bckd->bnkgqc', qb, k_c).astype(jnp.float32) * scale
    blk = jnp.arange(n_blk)[:, None, None]
    qpos = blk * BLOCK + jnp.arange(BLOCK)[None, :, None]
    kpos = (blk - 1) * BLOCK + jnp.arange(3 * BLOCK)[None, None, :]
    valid = (jnp.abs(qpos - kpos) <= WINDOW) & (kpos >= 0) & (kpos < n_tok)
    s_loc = jnp.where(valid[None, :, None, None], s_loc, NEG_INF)
    sink_col = jnp.broadcast_to(sink_kg[None, None, :, :, None, None], s_loc.shape[:-1] + (1,))
    p = jax.nn.softmax(jnp.concatenate([s_loc, s_ctx, sink_col], axis=-1), axis=-1).astype(v.dtype)
    n_loc = 3 * BLOCK
    o = (jnp.einsum('bnkgqs,bnskd->bnqkgd', p[..., :n_loc], vb)
         + jnp.einsum('bnkgqc,bckd->bnqkgd', p[..., n_loc:n_loc + n_ctx], v_c))
    y = o.reshape(bsz, n_tok, q_cols) @ w_o
    if not need_ctx_out:
        return y, None
    q_c = (hc @ w_qkv[:, :q_cols]).reshape(bsz, n_ctx, N_KV_HEADS, GQA_GROUP, HEAD_DIM)
    s_c = jnp.einsum('bqkgd,bckd->bkgqc', q_c, k_c).astype(jnp.float32) * scale
    sink_c = jnp.broadcast_to(sink_kg[None, :, :, None, None], s_c.shape[:-1] + (1,))
    p_c = jax.nn.softmax(jnp.concatenate([s_c, sink_c], axis=-1), axis=-1).astype(v_c.dtype)
    o_c = jnp.einsum('bkgqc,bckd->bqkgd', p_c[..., :n_ctx], v_c)
    yc = o_c.reshape(bsz, n_ctx, q_cols) @ w_o
    return y, yc


def block_diag_linear(u, w, b):
    t = u.reshape(*u.shape[:-1], LRU_BLOCKS, LRU_BLOCK_SIZE)
    return jnp.einsum('btnd,nde->btne', t, w).reshape(u.shape) + b


def linear_scan(a, b, h0, reverse):
    if h0 is not None:
        idx = -1 if reverse else 0
        b = b.at[:, idx].add(a[:, idx] * h0)

    def combine(e1, e2):
        a1, b1 = e1
        a2, b2 = e2
        return a1 * a2, a2 * b1 + b2

    _, hs = lax.associative_scan(combine, (a, b), reverse=reverse, axis=1)
    return hs


def rglru_mixer(h, hc, w_in, conv_w, conv_b, wa, ba, wx, bx, lam, w_out, need_ctx_out):
    gate_l, xb_l = jnp.split(h @ w_in, 2, axis=-1)
    if need_ctx_out:
        gate_c, xb_c = jnp.split(hc @ w_in, 2, axis=-1)
    else:
        gate_c, xb_c = None, hc @ w_in[:, LRU_WIDTH:]
    sum_l = None
    sum_c = None
    for d in range(2):
        rev = d == 1
        pad = (0, LRU_CONV - 1) if rev else (LRU_CONV - 1, 0)

        def gates(u):
            uc = depthwise_conv(u, conv_w[d], conv_b[d], pad[0], pad[1])
            r = jax.nn.sigmoid(block_diag_linear(uc, wa[d], ba[d])).astype(jnp.float32)
            ig = jax.nn.sigmoid(block_diag_linear(uc, wx[d], bx[d])).astype(jnp.float32)
            log_a = -LRU_C * r * jax.nn.softplus(-lam[d].astype(jnp.float32))
            a = jnp.exp(log_a)
            bterm = jnp.sqrt(-jnp.expm1(2.0 * log_a)) * (ig * uc.astype(jnp.float32))
            return a, bterm

        a_c, b_c = gates(xb_c)
        hs_c = linear_scan(a_c, b_c, None, rev)
        h_end = hs_c[:, 0] if rev else hs_c[:, -1]
        a_l, b_l = gates(xb_l)
        hs_l = linear_scan(a_l, b_l, h_end, rev)
        sum_l = hs_l if sum_l is None else sum_l + hs_l
        sum_c = hs_c if sum_c is None else sum_c + hs_c
    y = (jax.nn.gelu(gate_l) * sum_l.astype(h.dtype)) @ w_out
    if not need_ctx_out:
        return y, None
    yc = (jax.nn.gelu(gate_c) * sum_c.astype(hc.dtype)) @ w_out
    return y, yc


def conformer_conv_mixer(u, w_in, b_in, dw_w, dw_b, ln_g, ln_b, w_out, b_out):
    z = u @ w_in + b_in
    z = z[..., :D_MODEL] * jax.nn.sigmoid(z[..., D_MODEL:])
    half = CONF_WIDTH // 2
    z = depthwise_conv(z, dw_w, dw_b, half, half)
    z = jax.nn.silu(layer_norm(z, ln_g, ln_b))
    return z @ w_out + b_out


def conv_ffn(u, w_up, conv_w, conv_b, w_down):
    g, v = jnp.split(u @ w_up, 2, axis=-1)
    g = depthwise_conv(g, conv_w, conv_b, FFN_CONV // 2, FFN_CONV // 2)
    return (jax.nn.silu(g) * v) @ w_down


def setup_inputs(seed: int = 0) -> dict:
    key = jax.random.key(seed)
    ks = jax.random.split(key, 40)
    f32 = jnp.float32
    D, F, R = D_MODEL, D_FF, LRU_WIDTH

    def nrm(i, shape, scale):
        return jax.random.normal(ks[i], shape, f32) * scale

    qkv_cols = (N_HEADS + 2 * N_KV_HEADS) * HEAD_DIM
    u = jax.random.uniform(ks[0], (N_LRU_LAYERS, 2, R), f32, 0.9, 0.999)
    s = u ** (1.0 / LRU_C)
    lam = jnp.log(s) - jnp.log1p(-s)
    return {
        "x": nrm(1, (BATCH, SEQ, D), 1.0),
        "c": nrm(2, (BATCH, D), 1.0),
        "ctx": nrm(3, (BATCH, CTX_LEN, D), 1.0),
        "c_ctx": nrm(4, (D,), 1.0),
        "ada_w": nrm(5, (DEPTH, D, 6 * D), 0.5 * D ** -0.5),
        "ada_b": nrm(6, (DEPTH, 6 * D), 0.02),
        "norm_mix_g": 1.0 + nrm(7, (DEPTH, D), 0.02),
        "norm_ffn_g": 1.0 + nrm(8, (DEPTH, D), 0.02),
        "attn_w_qkv": nrm(9, (N_ATTN_LAYERS, D, qkv_cols), D ** -0.5),
        "attn_w_o": nrm(10, (N_ATTN_LAYERS, N_HEADS * HEAD_DIM, D), (N_HEADS * HEAD_DIM) ** -0.5),
        "attn_sink": nrm(11, (N_ATTN_LAYERS, N_HEADS), 0.5),
        "lru_w_in": nrm(12, (N_LRU_LAYERS, D, 2 * R), D ** -0.5),
        "lru_conv_w": nrm(13, (N_LRU_LAYERS, 2, LRU_CONV, R), LRU_CONV ** -0.5),
        "lru_conv_b": nrm(14, (N_LRU_LAYERS, 2, R), 0.02),
        "lru_wa": nrm(15, (N_LRU_LAYERS, 2, LRU_BLOCKS, LRU_BLOCK_SIZE, LRU_BLOCK_SIZE), LRU_BLOCK_SIZE ** -0.5),
        "lru_ba": nrm(16, (N_LRU_LAYERS, 2, R), 0.02),
        "lru_wx": nrm(17, (N_LRU_LAYERS, 2, LRU_BLOCKS, LRU_BLOCK_SIZE, LRU_BLOCK_SIZE), LRU_BLOCK_SIZE ** -0.5),
        "lru_bx": nrm(18, (N_LRU_LAYERS, 2, R), 0.02),
        "lru_lambda": lam,
        "lru_w_out": nrm(19, (N_LRU_LAYERS, R, D), R ** -0.5),
        "conf_w_in": nrm(20, (N_CONV_LAYERS, D, 2 * D), D ** -0.5),
        "conf_b_in": nrm(21, (N_CONV_LAYERS, 2 * D), 0.02),
        "conf_dw_w": nrm(22, (N_CONV_LAYERS, CONF_WIDTH, D), CONF_WIDTH ** -0.5),
        "conf_dw_b": nrm(23, (N_CONV_LAYERS, D), 0.02),
        "conf_ln_g": 1.0 + nrm(24, (N_CONV_LAYERS, D), 0.02),
        "conf_ln_b": nrm(25, (N_CONV_LAYERS, D), 0.02),
        "conf_w_out": nrm(26, (N_CONV_LAYERS, D, D), D ** -0.5),
        "conf_b_out": nrm(27, (N_CONV_LAYERS, D), 0.02),
        "ffn_w_up": nrm(28, (DEPTH, D, 2 * F), D ** -0.5),
        "ffn_conv_w": nrm(29, (DEPTH, FFN_CONV, F), FFN_CONV ** -0.5),
        "ffn_conv_b": nrm(30, (DEPTH, F), 0.02),
        "ffn_w_down": nrm(31, (DEPTH, F, D), F ** -0.5),
        "final_norm_g": 1.0 + nrm(32, (D,), 0.02),
    }


def reference(x, c, ctx, c_ctx, ada_w, ada_b, norm_mix_g, norm_ffn_g, attn_w_qkv, attn_w_o, attn_sink,
              lru_w_in, lru_conv_w, lru_conv_b, lru_wa, lru_ba, lru_wx, lru_bx, lru_lambda, lru_w_out,
              conf_w_in, conf_b_in, conf_dw_w, conf_dw_b, conf_ln_g, conf_ln_b, conf_w_out, conf_b_out,
              ffn_w_up, ffn_conv_w, ffn_conv_b, ffn_w_down, final_norm_g):
    cos, sin = axial_rope_tables(x.shape[1])
    xc = ctx
    c_ctx_row = c_ctx[None, :]
    for i in range(DEPTH):
        last = i == DEPTH - 1
        kind, j = i % N_MIXERS, i // N_MIXERS
        sh1, sc1, g1, sh2, sc2, g2 = ada_mod(c, ada_w[i], ada_b[i])
        h = modulate(x, norm_mix_g[i], sh1, sc1)
        ctx_used = (not last) or kind != 2
        if ctx_used:
            csh1, csc1, cg1, csh2, csc2, cg2 = ada_mod(c_ctx_row, ada_w[i], ada_b[i])
            hc = modulate(xc, norm_mix_g[i], csh1, csc1)
        if kind == 0:
            y, yc = attention_mixer(h, hc, attn_w_qkv[j], attn_w_o[j], attn_sink[j], cos, sin, not last)
        elif kind == 1:
            y, yc = rglru_mixer(h, hc, lru_w_in[j], lru_conv_w[j], lru_conv_b[j], lru_wa[j], lru_ba[j],
                                lru_wx[j], lru_bx[j], lru_lambda[j], lru_w_out[j], not last)
        else:
            conf_args = (conf_w_in[j], conf_b_in[j], conf_dw_w[j], conf_dw_b[j], conf_ln_g[j], conf_ln_b[j],
                         conf_w_out[j], conf_b_out[j])
            y = conformer_conv_mixer(h, *conf_args)
            yc = None if last else conformer_conv_mixer(hc, *conf_args)
        x = x + g1 * y
        x = x + g2 * conv_ffn(modulate(x, norm_ffn_g[i], sh2, sc2), ffn_w_up[i], ffn_conv_w[i], ffn_conv_b[i],
                              ffn_w_down[i])
        if not last:
            xc = xc + cg1 * yc
            xc = xc + cg2 * conv_ffn(modulate(xc, norm_ffn_g[i], csh2, csc2), ffn_w_up[i], ffn_conv_w[i],
                                     ffn_conv_b[i], ffn_w_down[i])
    return rms_norm(x, final_norm_g)
```

```python
import functools

import jax
import jax.numpy as jnp
from jax import lax
from jax.experimental import pallas as pl
from jax.experimental.pallas import tpu as pltpu

F32 = jnp.float32
BF16 = jnp.bfloat16

HEAD_DIM = 128
N_HEADS = 16
N_KV_HEADS = 4
GQA_GROUP = N_HEADS // N_KV_HEADS
ATT_BLOCK = 128
GRID_W = 64
ROPE_THETA = 10000.0
ROPE_FREQS = HEAD_DIM // 4
LRU_BLOCK = 128
LRU_C = 8.0
LRU_CONV = 4
CONF_WIDTH = 31
FFN_CONV = 3
NORM_EPS = 1e-6
NEG_INF = -1e30
N_MIXERS = 3

LANES = 128
SUBLANES = 8
MOD_ROWS = 8
VMEM_LIMIT = 56 * 1024 * 1024


def _cparams(sem):
    return pltpu.CompilerParams(dimension_semantics=sem, vmem_limit_bytes=VMEM_LIMIT)


def _tile(n, pref):
    t = min(n, pref)
    assert n % t == 0, (n, t)
    return t


def _silu(x):
    return x * jax.nn.sigmoid(x)


def _rms_mod(x, g, shift, scale):
    ms = jnp.mean(x * x, axis=-1, keepdims=True)
    return (x * lax.rsqrt(ms + NORM_EPS) * g) * (1.0 + scale) + shift


def _mod_slice(mod_ref, k, d):
    return mod_ref[:, k * d:(k + 1) * d]


def _ada_kernel(c_ref, w_ref, b_ref, o_ref):
    s = _silu(c_ref[...]).astype(BF16)
    o_ref[...] = jnp.dot(s, w_ref[...].astype(BF16), preferred_element_type=F32) + b_ref[...]


def _ada_call(cvec, ada_w, ada_b):
    depth, d, n = ada_w.shape
    tn = _tile(n, 1024)
    return pl.pallas_call(
        _ada_kernel,
        grid=(depth, n // tn),
        in_specs=[pl.BlockSpec((MOD_ROWS, d), lambda l, j: (0, 0)),
                  pl.BlockSpec((None, d, tn), lambda l, j: (l, 0, j)),
                  pl.BlockSpec((None, 1, tn), lambda l, j: (l, 0, j))],
        out_specs=pl.BlockSpec((None, MOD_ROWS, tn), lambda l, j: (l, 0, j)),
        out_shape=jax.ShapeDtypeStruct((depth, MOD_ROWS, n), F32),
        compiler_params=_cparams(("parallel", "parallel")),
        name="ada",
    )(cvec, ada_w, ada_b.reshape(depth, 1, n))


def _rope(y, cos, sp, sm):
    return y * cos + pltpu.roll(y, 32, 1) * sp + pltpu.roll(y, HEAD_DIM - 32, 1) * sm


def _proj_kernel(*refs, d, tn, has_bias, glu, rope_tiles):
    it = iter(refs)
    x_ref, mod_ref, g_ref = next(it), next(it), next(it)
    w_refs = [next(it) for _ in range(2 if glu else 1)]
    b_refs = [next(it) for _ in range((2 if glu else 1) if has_bias else 0)]
    rope_refs = [next(it) for _ in range(3 if rope_tiles else 0)]
    o_ref, h_ref = next(it), next(it)
    j = pl.program_id(2)

    @pl.when(j == 0)
    def _():
        h_ref[...] = _rms_mod(x_ref[...], g_ref[...], _mod_slice(mod_ref, 0, d),
                              _mod_slice(mod_ref, 1, d)).astype(BF16)

    h = h_ref[...]
    y = jnp.dot(h, w_refs[0][...], preferred_element_type=F32)
    if has_bias:
        y = y + b_refs[0][...]
    if glu:
        y2 = jnp.dot(h, w_refs[1][...], preferred_element_type=F32)
        if has_bias:
            y2 = y2 + b_refs[1][...]
        y = y * jax.nn.sigmoid(y2)
    if rope_tiles:
        @pl.when(j < rope_tiles)
        def _():
            cos, sp, sm = (r[...] for r in rope_refs)
            for c in range(tn // HEAD_DIM):
                sl = slice(c * HEAD_DIM, (c + 1) * HEAD_DIM)
                o_ref[:, sl] = _rope(y[:, sl], cos, sp, sm).astype(o_ref.dtype)

        @pl.when(j >= rope_tiles)
        def _():
            o_ref[...] = y.astype(o_ref.dtype)
    else:
        o_ref[...] = y.astype(o_ref.dtype)


def _proj_call(x, mod, mod_row, norm_g, w, bias=None, *, glu=False, rope=None, rope_cols=0,
               out_dtype=F32, tm_pref=512, tn_pref=512, name="proj"):
    nb, s, d = x.shape
    n_out = w.shape[1] // 2 if glu else w.shape[1]
    tm, tn = _tile(s, tm_pref), _tile(n_out, tn_pref)
    nj = n_out // tn
    mod_map = (lambda b, i, j: (b, 0, 0)) if mod_row is None else (lambda b, i, j: (mod_row, 0, 0))
    args = [x, mod, norm_g.reshape(1, d), w]
    specs = [pl.BlockSpec((None, tm, d), lambda b, i, j: (b, i, 0)),
             pl.BlockSpec((None, 1, mod.shape[-1]), mod_map),
             pl.BlockSpec((1, d), lambda b, i, j: (0, 0)),
             pl.BlockSpec((d, tn), lambda b, i, j: (0, j))]
    if glu:
        args.append(w)
        specs.append(pl.BlockSpec((d, tn), lambda b, i, j: (0, j + nj)))
    if bias is not None:
        b2 = bias.reshape(1, -1)
        args.append(b2)
        specs.append(pl.BlockSpec((1, tn), lambda b, i, j: (0, j)))
        if glu:
            args.append(b2)
            specs.append(pl.BlockSpec((1, tn), lambda b, i, j: (0, j + nj)))
    rope_tiles = 0
    if rope is not None:
        assert rope_cols % tn == 0
        rope_tiles = rope_cols // tn
        for t in rope:
            args.append(t)
            specs.append(pl.BlockSpec((tm, HEAD_DIM), lambda b, i, j: (i, 0)))
    kern = functools.partial(_proj_kernel, d=d, tn=tn, has_bias=bias is not None, glu=glu, rope_tiles=rope_tiles)
    return pl.pallas_call(
        kern,
        grid=(nb, s // tm, nj),
        in_specs=specs,
        out_specs=pl.BlockSpec((None, tm, tn), lambda b, i, j: (b, i, j)),
        out_shape=jax.ShapeDtypeStruct((nb, s, n_out), out_dtype),
        scratch_shapes=[pltpu.VMEM((tm, d), BF16)],
        compiler_params=_cparams(("parallel", "parallel", "arbitrary")),
        name=name,
    )(*args)


def _rope_tables(s):
    t = jnp.arange(s)
    row = (t // GRID_W).astype(F32)
    col = (t % GRID_W).astype(F32)
    freq = ROPE_THETA ** (-jnp.arange(ROPE_FREQS, dtype=F32) / ROPE_FREQS)
    dd = jnp.arange(HEAD_DIM)
    axis, half, f = dd // (2 * ROPE_FREQS), (dd % (2 * ROPE_FREQS)) // ROPE_FREQS, dd % ROPE_FREQS
    pos = jnp.where(axis[None, :] == 0, row[:, None], col[:, None])
    ang = pos * freq[f][None, :]
    cos, sin = jnp.cos(ang), jnp.sin(ang)
    sp = jnp.where(half[None, :] == 1, sin, 0.0)
    sm = jnp.where(half[None, :] == 0, -sin, 0.0)
    return cos, sp, sm


def _attn_heads(q_ref, k_parts, v_parts, sink_ref, o_ref, mask):
    tq = q_ref.shape[0]
    scale = HEAD_DIM ** -0.5
    grp = lax.broadcasted_iota(jnp.int32, (GQA_GROUP * tq, 1), 0) // tq
    for kh in range(N_KV_HEADS):
        ksl = slice(kh * HEAD_DIM, (kh + 1) * HEAD_DIM)
        kcat = jnp.concatenate([r[:, ksl] for r in k_parts], axis=0)
        vcat = jnp.concatenate([r[:, ksl] for r in v_parts], axis=0)
        qg = jnp.concatenate(
            [q_ref[:, (kh * GQA_GROUP + g) * HEAD_DIM:(kh * GQA_GROUP + g + 1) * HEAD_DIM] for g in range(GQA_GROUP)],
            axis=0)
        s = lax.dot_general(qg, kcat, (((1,), (1,)), ((), ())), preferred_element_type=F32) * scale
        if mask is not None:
            s = jnp.where(mask, s, NEG_INF)
        sink = jnp.full((GQA_GROUP * tq, 1), sink_ref[kh * GQA_GROUP], F32)
        for g in range(1, GQA_GROUP):
            sink = jnp.where(grp == g, sink_ref[kh * GQA_GROUP + g], sink)
        m = jnp.maximum(jnp.max(s, axis=-1, keepdims=True), sink)
        p = jnp.exp(s - m)
        l = jnp.sum(p, axis=-1, keepdims=True) + jnp.exp(sink - m)
        o = jnp.dot(p.astype(BF16), vcat, preferred_element_type=F32) / l
        for g in range(GQA_GROUP):
            h = kh * GQA_GROUP + g
            o_ref[:, h * HEAD_DIM:(h + 1) * HEAD_DIM] = o[g * tq:(g + 1) * tq].astype(o_ref.dtype)


def _attn_kernel(sink_ref, q_ref, kp_ref, kc_ref, kn_ref, vp_ref, vc_ref, vn_ref, kx_ref, vx_ref, o_ref):
    n = pl.program_id(1)
    nblk = pl.num_programs(1)
    tq = ATT_BLOCK
    n_loc, n_ctx = 3 * tq, kx_ref.shape[0]
    r = lax.broadcasted_iota(jnp.int32, (GQA_GROUP * tq, n_loc + n_ctx), 0) % tq
    col = lax.broadcasted_iota(jnp.int32, (GQA_GROUP * tq, n_loc + n_ctx), 1)
    lo = jnp.where(n == 0, tq, 0)
    hi = jnp.where(n == nblk - 1, 2 * tq, n_loc)
    local_ok = (col >= jnp.maximum(r, lo)) & (col <= r + 2 * tq) & (col < hi)
    mask = local_ok | (col >= n_loc)
    _attn_heads(q_ref, [kp_ref, kc_ref, kn_ref, kx_ref], [vp_ref, vc_ref, vn_ref, vx_ref], sink_ref, o_ref, mask)


def _attn_call(qkv, qkv_ctx, sink):
    b, s, _ = qkv.shape
    n_ctx = qkv_ctx.shape[1]
    tq = ATT_BLOCK
    nblk = s // tq
    qc, kc = N_HEADS * HEAD_DIM, N_KV_HEADS * HEAD_DIM
    kblk, vblk = qc // kc, qc // kc + 1
    kv_spec = lambda off, cb: pl.BlockSpec(
        (None, tq, kc), lambda bb, n: (bb, jnp.clip(n + off, 0, nblk - 1), cb))
    return pl.pallas_call(
        _attn_kernel,
        grid=(b, nblk),
        in_specs=[pl.BlockSpec(memory_space=pltpu.SMEM),
                  pl.BlockSpec((None, tq, qc), lambda bb, n: (bb, n, 0)),
                  kv_spec(-1, kblk), kv_spec(0, kblk), kv_spec(1, kblk),
                  kv_spec(-1, vblk), kv_spec(0, vblk), kv_spec(1, vblk),
                  pl.BlockSpec((None, n_ctx, kc), lambda bb, n: (bb, 0, kblk)),
                  pl.BlockSpec((None, n_ctx, kc), lambda bb, n: (bb, 0, vblk))],
        out_specs=pl.BlockSpec((None, tq, qc), lambda bb, n: (bb, n, 0)),
        out_shape=jax.ShapeDtypeStruct((b, s, qc), BF16),
        compiler_params=_cparams(("parallel", "parallel")),
        name="attn",
    )(sink, qkv, qkv, qkv, qkv, qkv, qkv, qkv, qkv_ctx, qkv_ctx)


def _attn_ctx_kernel(sink_ref, q_ref, kx_ref, vx_ref, o_ref):
    _attn_heads(q_ref, [kx_ref], [vx_ref], sink_ref, o_ref, None)


def _attn_ctx_call(qkv_ctx, sink):
    b, n_ctx, _ = qkv_ctx.shape
    qc, kc = N_HEADS * HEAD_DIM, N_KV_HEADS * HEAD_DIM
    tq = _tile(n_ctx, ATT_BLOCK)
    return pl.pallas_call(
        _attn_ctx_kernel,
        grid=(b, n_ctx // tq),
        in_specs=[pl.BlockSpec(memory_space=pltpu.SMEM),
                  pl.BlockSpec((None, tq, qc), lambda bb, n: (bb, n, 0)),
                  pl.BlockSpec((None, n_ctx, kc), lambda bb, n: (bb, 0, qc // kc)),
                  pl.BlockSpec((None, n_ctx, kc), lambda bb, n: (bb, 0, qc // kc + 1))],
        out_specs=pl.BlockSpec((None, tq, qc), lambda bb, n: (bb, n, 0)),
        out_shape=jax.ShapeDtypeStruct((b, n_ctx, qc), BF16),
        compiler_params=_cparams(("parallel", "parallel")),
        name="attn_ctx",
    )(sink, qkv_ctx, qkv_ctx, qkv_ctx)


def _oproj_kernel(a_ref, w_ref, x_ref, mod_ref, o_ref, *, d, gate_idx):
    y = jnp.dot(a_ref[...], w_ref[...], preferred_element_type=F32)
    o_ref[...] = x_ref[...] + _mod_slice(mod_ref, gate_idx, d) * y


def _oproj_call(a, w, x, mod, mod_row, *, gate_idx=2, tm_pref=512, name="oproj"):
    nb, s, d = x.shape
    k = a.shape[-1]
    tm = _tile(s, tm_pref)
    mod_map = (lambda b, i: (b, 0, 0)) if mod_row is None else (lambda b, i: (mod_row, 0, 0))
    return pl.pallas_call(
        functools.partial(_oproj_kernel, d=d, gate_idx=gate_idx),
        grid=(nb, s // tm),
        in_specs=[pl.BlockSpec((None, tm, k), lambda b, i: (b, i, 0)),
                  pl.BlockSpec((k, d), lambda b, i: (0, 0)),
                  pl.BlockSpec((None, tm, d), lambda b, i: (b, i, 0)),
                  pl.BlockSpec((None, 1, mod.shape[-1]), mod_map)],
        out_specs=pl.BlockSpec((None, tm, d), lambda b, i: (b, i, 0)),
        out_shape=jax.ShapeDtypeStruct((nb, s, d), F32),
        compiler_params=_cparams(("parallel", "parallel")),
        name=name,
    )(a, w, x, mod)


def _ffn_kernel(*refs, d, tm, final_norm):
    it = iter(refs)
    xp_ref, x_ref, xn_ref, mod_ref, g_ref = (next(it) for _ in range(5))
    wg_ref, wv_ref, cw_ref, cb_ref, wd_ref = (next(it) for _ in range(5))
    fg_ref = next(it) if final_norm else None
    o_ref, h_ref, acc_ref = next(it), next(it), next(it)
    i, f = pl.program_id(1), pl.program_id(2)
    n_i, n_f = pl.num_programs(1), pl.num_programs(2)
    halo = SUBLANES

    @pl.when(f == 0)
    def _():
        xs = jnp.concatenate([xp_ref[...], x_ref[...], xn_ref[...]], axis=0)
        h_ref[...] = _rms_mod(xs, g_ref[...], _mod_slice(mod_ref, 3, d), _mod_slice(mod_ref, 4, d)).astype(BF16)

    h = h_ref[...]
    gate = jnp.dot(h, wg_ref[...], preferred_element_type=F32)
    val = jnp.dot(h, wv_ref[...], preferred_element_type=F32)[halo:halo + tm]
    row = lax.broadcasted_iota(jnp.int32, (tm + 2 * halo, 1), 0)
    keep = ((row >= halo) | (i > 0)) & ((row < tm + halo) | (i < n_i - 1))
    gate = jnp.where(keep, gate, 0.0)
    cw = cw_ref[...]
    gc = cb_ref[...]
    for k in range(FFN_CONV):
        off = halo + k - FFN_CONV // 2
        gc = gc + cw[k:k + 1] * gate[off:off + tm]
    act = (_silu(gc) * val).astype(BF16)
    part = jnp.dot(act, wd_ref[...], preferred_element_type=F32)

    @pl.when(f == 0)
    def _():
        acc_ref[...] = part

    @pl.when(f > 0)
    def _():
        acc_ref[...] += part

    @pl.when(f == n_f - 1)
    def _():
        out = x_ref[...] + _mod_slice(mod_ref, 5, d) * acc_ref[...]
        if final_norm:
            ms = jnp.mean(out * out, axis=-1, keepdims=True)
            out = out * lax.rsqrt(ms + NORM_EPS) * fg_ref[...]
        o_ref[...] = out


def _ffn_call(x, mod, mod_row, norm_g, w_up, conv_w, conv_b, w_down, final_g=None, *, tm_pref=512, tf_pref=512,
              name="ffn"):
    nb, s, d = x.shape
    ff = w_down.shape[0]
    tm, tf = _tile(s, tm_pref), _tile(ff, tf_pref)
    n_i, n_f = s // tm, ff // tf
    r8 = tm // SUBLANES
    mod_map = (lambda b, i, f: (b, 0, 0)) if mod_row is None else (lambda b, i, f: (mod_row, 0, 0))
    args = [x, x, x, mod, norm_g.reshape(1, d), w_up, w_up, conv_w, conv_b.reshape(1, ff), w_down]
    specs = [pl.BlockSpec((None, SUBLANES, d), lambda b, i, f: (b, jnp.maximum(i * r8 - 1, 0), 0)),
             pl.BlockSpec((None, tm, d), lambda b, i, f: (b, i, 0)),
             pl.BlockSpec((None, SUBLANES, d), lambda b, i, f: (b, jnp.minimum((i + 1) * r8, s // SUBLANES - 1), 0)),
             pl.BlockSpec((None, 1, mod.shape[-1]), mod_map),
             pl.BlockSpec((1, d), lambda b, i, f: (0, 0)),
             pl.BlockSpec((d, tf), lambda b, i, f: (0, f)),
             pl.BlockSpec((d, tf), lambda b, i, f: (0, f + n_f)),
             pl.BlockSpec((FFN_CONV, tf), lambda b, i, f: (0, f)),
             pl.BlockSpec((1, tf), lambda b, i, f: (0, f)),
             pl.BlockSpec((tf, d), lambda b, i, f: (f, 0))]
    if final_g is not None:
        args.append(final_g.reshape(1, d))
        specs.append(pl.BlockSpec((1, d), lambda b, i, f: (0, 0)))
    return pl.pallas_call(
        functools.partial(_ffn_kernel, d=d, tm=tm, final_norm=final_g is not None),
        grid=(nb, n_i, n_f),
        in_specs=specs,
        out_specs=pl.BlockSpec((None, tm, d), lambda b, i, f: (b, i, 0)),
        out_shape=jax.ShapeDtypeStruct((nb, s, d), F32),
        scratch_shapes=[pltpu.VMEM((tm + 2 * SUBLANES, d), BF16), pltpu.VMEM((tm, d), F32)],
        compiler_params=_cparams(("parallel", "parallel", "arbitrary")),
        name=name,
    )(*args)


def _scan_chunk(a, b, hb, rev):
    tc, lanes = a.shape
    rowm = lax.broadcasted_iota(jnp.int32, (tc, lanes), 0) % SUBLANES
    for sft in (1, 2, 4):
        if rev:
            m, sh = rowm < SUBLANES - sft, tc - sft
        else:
            m, sh = rowm >= sft, sft
        a_s = jnp.where(m, pltpu.roll(a, sh, 0), 1.0)
        b_s = jnp.where(m, pltpu.roll(b, sh, 0), 0.0)
        b = a * b_s + b
        a = a * a_s
    n_g = tc // SUBLANES
    e = 0 if rev else SUBLANES - 1
    outs = [None] * n_g
    for g in (range(n_g - 1, -1, -1) if rev else range(n_g)):
        ag, bg = a[g * SUBLANES:(g + 1) * SUBLANES], b[g * SUBLANES:(g + 1) * SUBLANES]
        a_end = jnp.broadcast_to(ag[e:e + 1], ag.shape)
        b_end = jnp.broadcast_to(bg[e:e + 1], bg.shape)
        outs[g] = bg + ag * hb
        hb = b_end + a_end * hb
    return jnp.concatenate(outs, axis=0), hb


def _lru_kernel(gl_ref, xl_ref, gc_ref, xc_ref, cw_ref, cb_ref, wa_ref, ba_ref, wx_ref, bx_ref, lam_ref,
                zl_ref, zc_ref, xpl_ref, xpc_ref, hl_ref, hc_ref, *, tc):
    s, n_ctx = xl_ref.shape[0], xc_ref.shape[0]
    lanes = xl_ref.shape[1]
    pad = SUBLANES
    zpad = jnp.zeros((pad, lanes), F32)
    for xp_ref, x_ref, n in ((xpl_ref, xl_ref, s), (xpc_ref, xc_ref, n_ctx)):
        xp_ref[0:pad] = zpad
        xp_ref[pad + n:2 * pad + n] = zpad
        xp_ref[pad:pad + n] = x_ref[...]

    for dr in range(2):
        rev = dr == 1
        cw, cb = cw_ref[dr], cb_ref[dr]
        wa, ba, wx, bx = wa_ref[dr], ba_ref[dr], wx_ref[dr], bx_ref[dr]
        nl = -lam_ref[dr]
        sp = jnp.maximum(nl, 0.0) + jnp.log1p(jnp.exp(-jnp.abs(nl)))
        off0 = pad if rev else pad - (LRU_CONV - 1)

        def pass_(xp_ref, g_ref, hsum_ref, z_ref, n, hb):
            nch = n // tc

            def body(ci, hb):
                t0 = pl.multiple_of((nch - 1 - ci if rev else ci) * tc, tc)
                win = xp_ref[pl.ds(t0, tc + 2 * pad), :]
                uc = cb
                for k in range(LRU_CONV):
                    uc = uc + cw[k:k + 1] * win[off0 + k:off0 + k + tc]
                ucb = uc.astype(BF16)
                r = jax.nn.sigmoid(jnp.dot(ucb, wa, preferred_element_type=F32) + ba)
                ig = jax.nn.sigmoid(jnp.dot(ucb, wx, preferred_element_type=F32) + bx)
                log_a = (-LRU_C) * r * sp
                a = jnp.exp(log_a)
                th = jnp.tanh(log_a)
                bt = jnp.sqrt(-2.0 * th / (1.0 - th)) * (ig * uc)
                hs, hb = _scan_chunk(a, bt, hb, rev)
                if not rev:
                    hsum_ref[pl.ds(t0, tc), :] = hs
                else:
                    tot = hsum_ref[pl.ds(t0, tc), :] + hs
                    z_ref[pl.ds(t0, tc), :] = (jax.nn.gelu(g_ref[pl.ds(t0, tc), :]) * tot).astype(z_ref.dtype)
                return hb

            return lax.fori_loop(0, nch, body, hb)

        hb = pass_(xpc_ref, gc_ref, hc_ref, zc_ref, n_ctx, jnp.zeros((SUBLANES, lanes), F32))
        pass_(xpl_ref, gl_ref, hl_ref, zl_ref, s, hb)


def _lru_call(u, u_ctx, conv_w, conv_b, wa, ba, wx, bx, lam):
    b, s, r2 = u.shape
    r = r2 // 2
    n_ctx = u_ctx.shape[1]
    lanes = LRU_BLOCK
    nblk = r // lanes
    tc = _tile(n_ctx, 128)
    assert s % tc == 0
    vec = lambda v: v.reshape(2, 1, r)
    col = lambda n, off: pl.BlockSpec((None, n, lanes), lambda bb, c: (bb, 0, c + off))
    par = lambda rows: pl.BlockSpec((2, rows, lanes), lambda bb, c: (0, 0, c))
    wsp = pl.BlockSpec((2, None, lanes, lanes), lambda bb, c: (0, c, 0, 0))
    return pl.pallas_call(
        functools.partial(_lru_kernel, tc=tc),
        grid=(b, nblk),
        in_specs=[col(s, 0), col(s, nblk), col(n_ctx, 0), col(n_ctx, nblk),
                  par(LRU_CONV), par(1), wsp, par(1), wsp, par(1), par(1)],
        out_specs=[col(s, 0), col(n_ctx, 0)],
        out_shape=[jax.ShapeDtypeStruct((b, s, r), BF16), jax.ShapeDtypeStruct((b, n_ctx, r), BF16)],
        scratch_shapes=[pltpu.VMEM((s + 2 * SUBLANES, lanes), F32), pltpu.VMEM((n_ctx + 2 * SUBLANES, lanes), F32),
                        pltpu.VMEM((s, lanes), F32), pltpu.VMEM((n_ctx, lanes), F32)],
        compiler_params=_cparams(("parallel", "parallel")),
        name="lru",
    )(u, u, u_ctx, u_ctx, conv_w, vec(conv_b), wa, vec(ba), wx, vec(bx), vec(lam))


CONF_HALO = 16
CONF_ROWS = 64


def _conf_kernel(zp_ref, z_ref, zn_ref, x_ref, mod_ref, dw_ref, db_ref, lg_ref, lb_ref, wo_ref, bo_ref,
                 o_ref, zs_ref, y_ref, *, d, tm):
    i, n_i = pl.program_id(1), pl.num_programs(1)
    zs_ref[0:CONF_HALO] = jnp.where(i > 0, zp_ref[...], 0.0)
    zs_ref[CONF_HALO:CONF_HALO + tm] = z_ref[...]
    zs_ref[CONF_HALO + tm:2 * CONF_HALO + tm] = jnp.where(i < n_i - 1, zn_ref[...], 0.0)
    base = CONF_HALO - CONF_WIDTH // 2
    rows = min(CONF_ROWS, tm)

    def lane_chunk(c, carry):
        l0 = pl.multiple_of(c * LANES, LANES)
        w = dw_ref[:, pl.ds(l0, LANES)]
        bias = db_ref[:, pl.ds(l0, LANES)]
        for r0 in range(0, tm, rows):
            acc = jnp.broadcast_to(bias, (rows, LANES))
            for k in range(CONF_WIDTH):
                acc = acc + w[k:k + 1] * zs_ref[base + r0 + k:base + r0 + k + rows, pl.ds(l0, LANES)]
            y_ref[r0:r0 + rows, pl.ds(l0, LANES)] = acc
        return carry

    lax.fori_loop(0, d // LANES, lane_chunk, 0)
    y = y_ref[...]
    mu = jnp.mean(y, axis=-1, keepdims=True)
    yc = y - mu
    var = jnp.mean(yc * yc, axis=-1, keepdims=True)
    y = _silu(yc * lax.rsqrt(var + NORM_EPS) * lg_ref[...] + lb_ref[...])
    out = jnp.dot(y.astype(BF16), wo_ref[...], preferred_element_type=F32) + bo_ref[...]
    o_ref[...] = x_ref[...] + _mod_slice(mod_ref, 2, d) * out


def _conf_call(z, x, mod, mod_row, dw_w, dw_b, ln_g, ln_b, w_out, b_out, *, tm_pref=256, name="conf"):
    nb, s, d = x.shape
    tm = _tile(s, tm_pref)
    rh = tm // CONF_HALO
    row = lambda v: v.reshape(1, d)
    mod_map = (lambda b, i: (b, 0, 0)) if mod_row is None else (lambda b, i: (mod_row, 0, 0))
    const = lambda shape: pl.BlockSpec(shape, lambda b, i: (0, 0))
    return pl.pallas_call(
        functools.partial(_conf_kernel, d=d, tm=tm),
        grid=(nb, s // tm),
        in_specs=[pl.BlockSpec((None, CONF_HALO, d), lambda b, i: (b, jnp.maximum(i * rh - 1, 0), 0)),
                  pl.BlockSpec((None, tm, d), lambda b, i: (b, i, 0)),
                  pl.BlockSpec((None, CONF_HALO, d),
                               lambda b, i: (b, jnp.minimum((i + 1) * rh, s // CONF_HALO - 1), 0)),
                  pl.BlockSpec((None, tm, d), lambda b, i: (b, i, 0)),
                  pl.BlockSpec((None, 1, mod.shape[-1]), mod_map),
                  const((CONF_WIDTH, d)), const((1, d)), const((1, d)), const((1, d)),
                  const((d, d)), const((1, d))],
        out_specs=pl.BlockSpec((None, tm, d), lambda b, i: (b, i, 0)),
        out_shape=jax.ShapeDtypeStruct((nb, s, d), F32),
        scratch_shapes=[pltpu.VMEM((tm + 2 * CONF_HALO, d), F32), pltpu.VMEM((tm, d), F32)],
        compiler_params=_cparams(("parallel", "parallel")),
        name=name,
    )(z, z, z, x, mod, dw_w, row(dw_b), row(ln_g), row(ln_b), w_out, row(b_out))


def kernel(x, c, ctx, c_ctx, ada_w, ada_b, norm_mix_g, norm_ffn_g, attn_w_qkv, attn_w_o, attn_sink, lru_w_in, lru_conv_w, lru_conv_b, lru_wa, lru_ba, lru_wx, lru_bx, lru_lambda, lru_w_out, conf_w_in, conf_b_in, conf_dw_w, conf_dw_b, conf_ln_g, conf_ln_b, conf_w_out, conf_b_out, ffn_w_up, ffn_conv_w, ffn_conv_b, ffn_w_down, final_norm_g):
    bsz, seq, d = x.shape
    depth = ada_w.shape[0]
    assert bsz + 1 <= MOD_ROWS
    ctx_row = bsz
    bf = lambda w: w.astype(BF16)

    cvec = jnp.zeros((MOD_ROWS, d), F32).at[:bsz].set(c).at[ctx_row].set(c_ctx)
    mods = _ada_call(cvec, ada_w, ada_b)
    rope = _rope_tables(seq)
    q_cols = N_HEADS * HEAD_DIM
    k_cols = N_KV_HEADS * HEAD_DIM
    xc = ctx

    for i in range(depth):
        last = i == depth - 1
        kind, j = i % N_MIXERS, i // N_MIXERS
        mod = mods[i][:, None, :]
        g_mix = norm_mix_g[i]
        ctx_used = (not last) or kind != 2
        if kind == 0:
            w_qkv, w_o = bf(attn_w_qkv[j]), bf(attn_w_o[j])
            qkv = _proj_call(x, mod, None, g_mix, w_qkv, rope=rope, rope_cols=q_cols + k_cols, out_dtype=BF16,
                             name="qkv")
            qkv_c = _proj_call(xc, mod, ctx_row, g_mix, w_qkv, out_dtype=BF16, name="qkv_ctx")
            o = _attn_call(qkv, qkv_c, attn_sink[j])
            x = _oproj_call(o, w_o, x, mod, None, name="attn_out")
            if not last:
                o_c = _attn_ctx_call(qkv_c, attn_sink[j])
                xc = _oproj_call(o_c, w_o, xc, mod, ctx_row, name="attn_out_ctx")
        elif kind == 1:
            w_in, w_out = bf(lru_w_in[j]), bf(lru_w_out[j])
            u = _proj_call(x, mod, None, g_mix, w_in, name="lru_in")
            u_c = _proj_call(xc, mod, ctx_row, g_mix, w_in, name="lru_in_ctx")
            z, z_c = _lru_call(u, u_c, lru_conv_w[j], lru_conv_b[j], bf(lru_wa[j]), lru_ba[j], bf(lru_wx[j]),
                               lru_bx[j], lru_lambda[j])
            x = _oproj_call(z, w_out, x, mod, None, name="lru_out")
            if not last:
                xc = _oproj_call(z_c, w_out, xc, mod, ctx_row, name="lru_out_ctx")
        else:
            w_in, w_out = bf(conf_w_in[j]), bf(conf_w_out[j])
            tail = (conf_dw_w[j], conf_dw_b[j], conf_ln_g[j], conf_ln_b[j], w_out, conf_b_out[j])
            zz = _proj_call(x, mod, None, g_mix, w_in, conf_b_in[j], glu=True, name="conf_in")
            x = _conf_call(zz, x, mod, None, *tail, name="conf_tail")
            if ctx_used and not last:
                zz_c = _proj_call(xc, mod, ctx_row, g_mix, w_in, conf_b_in[j], glu=True, name="conf_in_ctx")
                xc = _conf_call(zz_c, xc, mod, ctx_row, *tail, name="conf_tail_ctx")
        ffn_w = (norm_ffn_g[i], bf(ffn_w_up[i]), ffn_conv_w[i], ffn_conv_b[i], bf(ffn_w_down[i]))
        x = _ffn_call(x, mod, None, *ffn_w, final_g=final_norm_g if last else None, name="ffn")
        if not last:
            xc = _ffn_call(xc, mod, ctx_row, *ffn_w, name="ffn_ctx")
    return x
```

```python
import functools

import jax
import jax.numpy as jnp
from jax import lax
from jax.experimental import pallas as pl
from jax.experimental.pallas import tpu as pltpu

F32 = jnp.float32
BF16 = jnp.bfloat16

HEAD_DIM = 128
N_HEADS = 16
N_KV_HEADS = 4
GQA_GROUP = N_HEADS // N_KV_HEADS
ATT_BLOCK = 128
GRID_W = 64
ROPE_THETA = 10000.0
ROPE_FREQS = HEAD_DIM // 4
LRU_BLOCK = 128
LRU_C = 8.0
LRU_CONV = 4
CONF_WIDTH = 31
FFN_CONV = 3
NORM_EPS = 1e-6
NEG_INF = -1e30
N_MIXERS = 3

LANES = 128
SUBLANES = 8
MOD_ROWS = 8
VMEM_LIMIT = 56 * 1024 * 1024


def _cparams(sem):
    return pltpu.CompilerParams(dimension_semantics=sem, vmem_limit_bytes=VMEM_LIMIT)


def _tile(n, pref):
    t = min(n, pref)
    assert n % t == 0, (n, t)
    return t


def _silu(x):
    return x * jax.nn.sigmoid(x)


def _rms_mod(x, g, shift, scale):
    ms = jnp.mean(x * x, axis=-1, keepdims=True)
    return (x * lax.rsqrt(ms + NORM_EPS) * g) * (1.0 + scale) + shift


def _mod_slice(mod_ref, k, d):
    return mod_ref[:, k * d:(k + 1) * d]


def _ada_kernel(c_ref, w_ref, b_ref, o_ref):
    s = _silu(c_ref[...]).astype(BF16)
    o_ref[...] = jnp.dot(s, w_ref[...].astype(BF16), preferred_element_type=F32) + b_ref[...]


def _ada_call(cvec, ada_w, ada_b):
    depth, d, n = ada_w.shape
    tn = _tile(n, 1024)
    return pl.pallas_call(
        _ada_kernel,
        grid=(depth, n // tn),
        in_specs=[pl.BlockSpec((MOD_ROWS, d), lambda l, j: (0, 0)),
                  pl.BlockSpec((None, d, tn), lambda l, j: (l, 0, j)),
                  pl.BlockSpec((None, 1, tn), lambda l, j: (l, 0, j))],
        out_specs=pl.BlockSpec((None, MOD_ROWS, tn), lambda l, j: (l, 0, j)),
        out_shape=jax.ShapeDtypeStruct((depth, MOD_ROWS, n), F32),
        compiler_params=_cparams(("parallel", "parallel")),
        name="ada",
    )(cvec, ada_w, ada_b.reshape(depth, 1, n))


def _rope(y, cos, sp, sm):
    return y * cos + pltpu.roll(y, 32, 1) * sp + pltpu.roll(y, HEAD_DIM - 32, 1) * sm


def _proj_kernel(*refs, d, tn, n_out, has_bias, glu, rope_cols):
    it = iter(refs)
    x_ref, mod_ref, g_ref, w_ref = (next(it) for _ in range(4))
    b_ref = next(it) if has_bias else None
    rope_refs = [next(it) for _ in range(3 if rope_cols else 0)]
    o_ref = next(it)
    h = _rms_mod(x_ref[...], g_ref[...], _mod_slice(mod_ref, 0, d), _mod_slice(mod_ref, 1, d)).astype(BF16)
    if rope_cols:
        cos, sp, sm = (r[...] for r in rope_refs)
    for c0 in range(0, n_out, tn):
        y = jnp.dot(h, w_ref[:, c0:c0 + tn], preferred_element_type=F32)
        if has_bias:
            y = y + b_ref[:, c0:c0 + tn]
        if glu:
            y2 = jnp.dot(h, w_ref[:, n_out + c0:n_out + c0 + tn], preferred_element_type=F32)
            if has_bias:
                y2 = y2 + b_ref[:, n_out + c0:n_out + c0 + tn]
            y = y * jax.nn.sigmoid(y2)
        if c0 < rope_cols:
            for c in range(0, tn, HEAD_DIM):
                o_ref[:, c0 + c:c0 + c + HEAD_DIM] = _rope(y[:, c:c + HEAD_DIM], cos, sp, sm).astype(o_ref.dtype)
        else:
            o_ref[:, c0:c0 + tn] = y.astype(o_ref.dtype)


def _proj_call(x, mod, mod_row, norm_g, w, bias=None, *, glu=False, rope=None, rope_cols=0,
               out_dtype=F32, tm_pref=512, tn_pref=512, name="proj"):
    nb, s, d = x.shape
    n_w = w.shape[1]
    n_out = n_w // 2 if glu else n_w
    tm, tn = _tile(s, tm_pref), _tile(n_out, tn_pref)
    mod_map = (lambda b, i: (b, 0, 0)) if mod_row is None else (lambda b, i: (mod_row, 0, 0))
    const = lambda shape: pl.BlockSpec(shape, lambda b, i: (0, 0), pipeline_mode=pl.Buffered(1))
    args = [x, mod, norm_g.reshape(1, d), w]
    specs = [pl.BlockSpec((None, tm, d), lambda b, i: (b, i, 0)),
             pl.BlockSpec((None, 1, mod.shape[-1]), mod_map),
             const((1, d)), const((d, n_w))]
    if bias is not None:
        args.append(bias.reshape(1, n_w))
        specs.append(const((1, n_w)))
    if rope is not None:
        assert rope_cols % tn == 0
        for t in rope:
            args.append(t)
            specs.append(pl.BlockSpec((tm, HEAD_DIM), lambda b, i: (i, 0)))
    kern = functools.partial(_proj_kernel, d=d, tn=tn, n_out=n_out, has_bias=bias is not None, glu=glu,
                             rope_cols=rope_cols if rope is not None else 0)
    return pl.pallas_call(
        kern,
        grid=(nb, s // tm),
        in_specs=specs,
        out_specs=pl.BlockSpec((None, tm, n_out), lambda b, i: (b, i, 0)),
        out_shape=jax.ShapeDtypeStruct((nb, s, n_out), out_dtype),
        compiler_params=_cparams(("parallel", "parallel")),
        name=name,
    )(*args)


def _rope_tables(s):
    t = jnp.arange(s)
    row = (t // GRID_W).astype(F32)
    col = (t % GRID_W).astype(F32)
    freq = ROPE_THETA ** (-jnp.arange(ROPE_FREQS, dtype=F32) / ROPE_FREQS)
    dd = jnp.arange(HEAD_DIM)
    axis, half, f = dd // (2 * ROPE_FREQS), (dd % (2 * ROPE_FREQS)) // ROPE_FREQS, dd % ROPE_FREQS
    pos = jnp.where(axis[None, :] == 0, row[:, None], col[:, None])
    ang = pos * freq[f][None, :]
    cos, sin = jnp.cos(ang), jnp.sin(ang)
    sp = jnp.where(half[None, :] == 1, sin, 0.0)
    sm = jnp.where(half[None, :] == 0, -sin, 0.0)
    return cos, sp, sm


def _attn_heads(q_ref, k_parts, v_parts, sink_ref, o_ref, mask):
    tq = q_ref.shape[0]
    scale = HEAD_DIM ** -0.5
    grp = lax.broadcasted_iota(jnp.int32, (GQA_GROUP * tq, 1), 0) // tq
    for kh in range(N_KV_HEADS):
        ksl = slice(kh * HEAD_DIM, (kh + 1) * HEAD_DIM)
        kcat = jnp.concatenate([r[:, ksl] for r in k_parts], axis=0)
        vcat = jnp.concatenate([r[:, ksl] for r in v_parts], axis=0)
        qg = jnp.concatenate(
            [q_ref[:, (kh * GQA_GROUP + g) * HEAD_DIM:(kh * GQA_GROUP + g + 1) * HEAD_DIM] for g in range(GQA_GROUP)],
            axis=0)
        s = lax.dot_general(qg, kcat, (((1,), (1,)), ((), ())), preferred_element_type=F32) * scale
        if mask is not None:
            s = jnp.where(mask, s, NEG_INF)
        sink = jnp.full((GQA_GROUP * tq, 1), sink_ref[kh * GQA_GROUP], F32)
        for g in range(1, GQA_GROUP):
            sink = jnp.where(grp == g, sink_ref[kh * GQA_GROUP + g], sink)
        m = jnp.maximum(jnp.max(s, axis=-1, keepdims=True), sink)
        p = jnp.exp(s - m)
        l = jnp.sum(p, axis=-1, keepdims=True) + jnp.exp(sink - m)
        o = jnp.dot(p.astype(BF16), vcat, preferred_element_type=F32) / l
        for g in range(GQA_GROUP):
            h = kh * GQA_GROUP + g
            o_ref[:, h * HEAD_DIM:(h + 1) * HEAD_DIM] = o[g * tq:(g + 1) * tq].astype(o_ref.dtype)


def _attn_kernel(sink_ref, q_ref, kp_ref, kc_ref, kn_ref, vp_ref, vc_ref, vn_ref, kx_ref, vx_ref, o_ref):
    n = pl.program_id(1)
    nblk = pl.num_programs(1)
    tq = ATT_BLOCK
    n_loc, n_ctx = 3 * tq, kx_ref.shape[0]
    r = lax.broadcasted_iota(jnp.int32, (GQA_GROUP * tq, n_loc + n_ctx), 0) % tq
    col = lax.broadcasted_iota(jnp.int32, (GQA_GROUP * tq, n_loc + n_ctx), 1)
    lo = jnp.where(n == 0, tq, 0)
    hi = jnp.where(n == nblk - 1, 2 * tq, n_loc)
    local_ok = (col >= jnp.maximum(r, lo)) & (col <= r + 2 * tq) & (col < hi)
    mask = local_ok | (col >= n_loc)
    _attn_heads(q_ref, [kp_ref, kc_ref, kn_ref, kx_ref], [vp_ref, vc_ref, vn_ref, vx_ref], sink_ref, o_ref, mask)


def _attn_call(qkv, qkv_ctx, sink):
    b, s, _ = qkv.shape
    n_ctx = qkv_ctx.shape[1]
    tq = ATT_BLOCK
    nblk = s // tq
    qc, kc = N_HEADS * HEAD_DIM, N_KV_HEADS * HEAD_DIM
    kblk, vblk = qc // kc, qc // kc + 1
    kv_spec = lambda off, cb: pl.BlockSpec(
        (None, tq, kc), lambda bb, n: (bb, jnp.clip(n + off, 0, nblk - 1), cb))
    return pl.pallas_call(
        _attn_kernel,
        grid=(b, nblk),
        in_specs=[pl.BlockSpec(memory_space=pltpu.SMEM),
                  pl.BlockSpec((None, tq, qc), lambda bb, n: (bb, n, 0)),
                  kv_spec(-1, kblk), kv_spec(0, kblk), kv_spec(1, kblk),
                  kv_spec(-1, vblk), kv_spec(0, vblk), kv_spec(1, vblk),
                  pl.BlockSpec((None, n_ctx, kc), lambda bb, n: (bb, 0, kblk)),
                  pl.BlockSpec((None, n_ctx, kc), lambda bb, n: (bb, 0, vblk))],
        out_specs=pl.BlockSpec((None, tq, qc), lambda bb, n: (bb, n, 0)),
        out_shape=jax.ShapeDtypeStruct((b, s, qc), BF16),
        compiler_params=_cparams(("parallel", "parallel")),
        name="attn",
    )(sink, qkv, qkv, qkv, qkv, qkv, qkv, qkv, qkv_ctx, qkv_ctx)


def _attn_ctx_kernel(sink_ref, q_ref, kx_ref, vx_ref, o_ref):
    _attn_heads(q_ref, [kx_ref], [vx_ref], sink_ref, o_ref, None)


def _attn_ctx_call(qkv_ctx, sink):
    b, n_ctx, _ = qkv_ctx.shape
    qc, kc = N_HEADS * HEAD_DIM, N_KV_HEADS * HEAD_DIM
    tq = _tile(n_ctx, ATT_BLOCK)
    return pl.pallas_call(
        _attn_ctx_kernel,
        grid=(b, n_ctx // tq),
        in_specs=[pl.BlockSpec(memory_space=pltpu.SMEM),
                  pl.BlockSpec((None, tq, qc), lambda bb, n: (bb, n, 0)),
                  pl.BlockSpec((None, n_ctx, kc), lambda bb, n: (bb, 0, qc // kc)),
                  pl.BlockSpec((None, n_ctx, kc), lambda bb, n: (bb, 0, qc // kc + 1))],
        out_specs=pl.BlockSpec((None, tq, qc), lambda bb, n: (bb, n, 0)),
        out_shape=jax.ShapeDtypeStruct((b, n_ctx, qc), BF16),
        compiler_params=_cparams(("parallel", "parallel")),
        name="attn_ctx",
    )(sink, qkv_ctx, qkv_ctx, qkv_ctx)


def _oproj_kernel(a_ref, w_ref, x_ref, mod_ref, o_ref, *, d, gate_idx):
    y = jnp.dot(a_ref[...], w_ref[...], preferred_element_type=F32)
    o_ref[...] = x_ref[...] + _mod_slice(mod_ref, gate_idx, d) * y


def _oproj_call(a, w, x, mod, mod_row, *, gate_idx=2, tm_pref=512, name="oproj"):
    nb, s, d = x.shape
    k = a.shape[-1]
    tm = _tile(s, tm_pref)
    mod_map = (lambda b, i: (b, 0, 0)) if mod_row is None else (lambda b, i: (mod_row, 0, 0))
    return pl.pallas_call(
        functools.partial(_oproj_kernel, d=d, gate_idx=gate_idx),
        grid=(nb, s // tm),
        in_specs=[pl.BlockSpec((None, tm, k), lambda b, i: (b, i, 0)),
                  pl.BlockSpec((k, d), lambda b, i: (0, 0)),
                  pl.BlockSpec((None, tm, d), lambda b, i: (b, i, 0)),
                  pl.BlockSpec((None, 1, mod.shape[-1]), mod_map)],
        out_specs=pl.BlockSpec((None, tm, d), lambda b, i: (b, i, 0)),
        out_shape=jax.ShapeDtypeStruct((nb, s, d), F32),
        compiler_params=_cparams(("parallel", "parallel")),
        name=name,
    )(a, w, x, mod)


def _ffn_kernel(*refs, d, tm, final_norm):
    it = iter(refs)
    xp_ref, x_ref, xn_ref, mod_ref, g_ref = (next(it) for _ in range(5))
    wg_ref, wv_ref, cw_ref, cb_ref, wd_ref = (next(it) for _ in range(5))
    fg_ref = next(it) if final_norm else None
    o_ref, h_ref, acc_ref = next(it), next(it), next(it)
    i, f = pl.program_id(1), pl.program_id(2)
    n_i, n_f = pl.num_programs(1), pl.num_programs(2)
    halo = SUBLANES

    @pl.when(f == 0)
    def _():
        xs = jnp.concatenate([xp_ref[...], x_ref[...], xn_ref[...]], axis=0)
        h_ref[...] = _rms_mod(xs, g_ref[...], _mod_slice(mod_ref, 3, d), _mod_slice(mod_ref, 4, d)).astype(BF16)
        acc_ref[...] = jnp.zeros_like(acc_ref)

    h = h_ref[...]
    gate = jnp.dot(h, wg_ref[...], preferred_element_type=F32)
    val = jnp.dot(h, wv_ref[...], preferred_element_type=F32)[halo:halo + tm]
    row = lax.broadcasted_iota(jnp.int32, (tm + 2 * halo, 1), 0)
    keep = ((row >= halo) | (i > 0)) & ((row < tm + halo) | (i < n_i - 1))
    gate = jnp.where(keep, gate, 0.0)
    cw = cw_ref[...]
    gc = cb_ref[...]
    for k in range(FFN_CONV):
        off = halo + k - FFN_CONV // 2
        gc = gc + cw[k:k + 1] * gate[off:off + tm]
    act = (_silu(gc) * val).astype(BF16)
    part = jnp.dot(act, wd_ref[...], preferred_element_type=F32)

    acc_ref[...] += part

    @pl.when(f == n_f - 1)
    def _():
        out = x_ref[...] + _mod_slice(mod_ref, 5, d) * acc_ref[...]
        if final_norm:
            ms = jnp.mean(out * out, axis=-1, keepdims=True)
            out = out * lax.rsqrt(ms + NORM_EPS) * fg_ref[...]
        o_ref[...] = out


def _ffn_call(x, mod, mod_row, norm_g, w_up, conv_w, conv_b, w_down, final_g=None, *, tm_pref=512, tf_pref=512,
              name="ffn"):
    nb, s, d = x.shape
    ff = w_down.shape[0]
    tm, tf = _tile(s, tm_pref), _tile(ff, tf_pref)
    n_i, n_f = s // tm, ff // tf
    r8 = tm // SUBLANES
    mod_map = (lambda b, i, f: (b, 0, 0)) if mod_row is None else (lambda b, i, f: (mod_row, 0, 0))
    args = [x, x, x, mod, norm_g.reshape(1, d), w_up, w_up, conv_w, conv_b.reshape(1, ff), w_down]
    specs = [pl.BlockSpec((None, SUBLANES, d), lambda b, i, f: (b, jnp.maximum(i * r8 - 1, 0), 0)),
             pl.BlockSpec((None, tm, d), lambda b, i, f: (b, i, 0)),
             pl.BlockSpec((None, SUBLANES, d), lambda b, i, f: (b, jnp.minimum((i + 1) * r8, s // SUBLANES - 1), 0)),
             pl.BlockSpec((None, 1, mod.shape[-1]), mod_map),
             pl.BlockSpec((1, d), lambda b, i, f: (0, 0)),
             pl.BlockSpec((d, tf), lambda b, i, f: (0, f)),
             pl.BlockSpec((d, tf), lambda b, i, f: (0, f + n_f)),
             pl.BlockSpec((FFN_CONV, tf), lambda b, i, f: (0, f)),
             pl.BlockSpec((1, tf), lambda b, i, f: (0, f)),
             pl.BlockSpec((tf, d), lambda b, i, f: (f, 0))]
    if final_g is not None:
        args.append(final_g.reshape(1, d))
        specs.append(pl.BlockSpec((1, d), lambda b, i, f: (0, 0)))
    return pl.pallas_call(
        functools.partial(_ffn_kernel, d=d, tm=tm, final_norm=final_g is not None),
        grid=(nb, n_i, n_f),
        in_specs=specs,
        out_specs=pl.BlockSpec((None, tm, d), lambda b, i, f: (b, i, 0)),
        out_shape=jax.ShapeDtypeStruct((nb, s, d), F32),
        scratch_shapes=[pltpu.VMEM((tm + 2 * SUBLANES, d), BF16), pltpu.VMEM((tm, d), F32)],
        compiler_params=_cparams(("parallel", "parallel", "arbitrary")),
        name=name,
    )(*args)


def _scan_chunk(a, b, hb, rev):
    tc, lanes = a.shape
    rowm = lax.broadcasted_iota(jnp.int32, (tc, lanes), 0) % SUBLANES
    for sft in (1, 2, 4):
        if rev:
            m, sh = rowm < SUBLANES - sft, tc - sft
        else:
            m, sh = rowm >= sft, sft
        a_s = jnp.where(m, pltpu.roll(a, sh, 0), 1.0)
        b_s = jnp.where(m, pltpu.roll(b, sh, 0), 0.0)
        b = a * b_s + b
        a = a * a_s
    n_g = tc // SUBLANES
    e = 0 if rev else SUBLANES - 1
    outs = [None] * n_g
    for g in (range(n_g - 1, -1, -1) if rev else range(n_g)):
        ag, bg = a[g * SUBLANES:(g + 1) * SUBLANES], b[g * SUBLANES:(g + 1) * SUBLANES]
        a_end = jnp.broadcast_to(ag[e:e + 1], ag.shape)
        b_end = jnp.broadcast_to(bg[e:e + 1], bg.shape)
        outs[g] = bg + ag * hb
        hb = b_end + a_end * hb
    return jnp.concatenate(outs, axis=0), hb


def _lru_kernel(gl_ref, xl_ref, gc_ref, xc_ref, cw_ref, cb_ref, wa_ref, ba_ref, wx_ref, bx_ref, lam_ref,
                zl_ref, zc_ref, xpl_ref, xpc_ref, hl_ref, hc_ref, *, tc):
    s, n_ctx = xl_ref.shape[0], xc_ref.shape[0]
    lanes = xl_ref.shape[1]
    pad = SUBLANES
    zpad = jnp.zeros((pad, lanes), F32)
    for xp_ref, x_ref, n in ((xpl_ref, xl_ref, s), (xpc_ref, xc_ref, n_ctx)):
        xp_ref[0:pad] = zpad
        xp_ref[pad + n:2 * pad + n] = zpad
        xp_ref[pad:pad + n] = x_ref[...]

    for dr in range(2):
        rev = dr == 1
        cw, cb = cw_ref[dr], cb_ref[dr]
        wa, ba, wx, bx = wa_ref[dr], ba_ref[dr], wx_ref[dr], bx_ref[dr]
        nl = -lam_ref[dr]
        sp = jnp.maximum(nl, 0.0) + jnp.log1p(jnp.exp(-jnp.abs(nl)))
        off0 = pad if rev else pad - (LRU_CONV - 1)

        def pass_(xp_ref, g_ref, hsum_ref, z_ref, n, hb):
            nch = n // tc

            def body(ci, hb):
                t0 = pl.multiple_of((nch - 1 - ci if rev else ci) * tc, tc)
                win = xp_ref[pl.ds(t0, tc + 2 * pad), :]
                uc = cb
                for k in range(LRU_CONV):
                    uc = uc + cw[k:k + 1] * win[off0 + k:off0 + k + tc]
                ucb = uc.astype(BF16)
                r = jax.nn.sigmoid(jnp.dot(ucb, wa, preferred_element_type=F32) + ba)
                ig = jax.nn.sigmoid(jnp.dot(ucb, wx, preferred_element_type=F32) + bx)
                log_a = (-LRU_C) * r * sp
                a = jnp.exp(log_a)
                th = jnp.tanh(log_a)
                bt = jnp.sqrt(-2.0 * th / (1.0 - th)) * (ig * uc)
                hs, hb = _scan_chunk(a, bt, hb, rev)
                if not rev:
                    hsum_ref[pl.ds(t0, tc), :] = hs
                else:
                    tot = hsum_ref[pl.ds(t0, tc), :] + hs
                    z_ref[pl.ds(t0, tc), :] = (jax.nn.gelu(g_ref[pl.ds(t0, tc), :]) * tot).astype(z_ref.dtype)
                return hb

            return lax.fori_loop(0, nch, body, hb)

        hb = pass_(xpc_ref, gc_ref, hc_ref, zc_ref, n_ctx, jnp.zeros((SUBLANES, lanes), F32))
        pass_(xpl_ref, gl_ref, hl_ref, zl_ref, s, hb)


def _lru_call(u, u_ctx, conv_w, conv_b, wa, ba, wx, bx, lam):
    b, s, r2 = u.shape
    r = r2 // 2
    n_ctx = u_ctx.shape[1]
    lanes = LRU_BLOCK
    nblk = r // lanes
    tc = _tile(n_ctx, 128)
    assert s % tc == 0
    vec = lambda v: v.reshape(2, 1, r)
    col = lambda n, off: pl.BlockSpec((None, n, lanes), lambda bb, c: (bb, 0, c + off))
    par = lambda rows: pl.BlockSpec((2, rows, lanes), lambda bb, c: (0, 0, c))
    wsp = pl.BlockSpec((2, None, lanes, lanes), lambda bb, c: (0, c, 0, 0))
    return pl.pallas_call(
        functools.partial(_lru_kernel, tc=tc),
        grid=(b, nblk),
        in_specs=[col(s, 0), col(s, nblk), col(n_ctx, 0), col(n_ctx, nblk),
                  par(LRU_CONV), par(1), wsp, par(1), wsp, par(1), par(1)],
        out_specs=[col(s, 0), col(n_ctx, 0)],
        out_shape=[jax.ShapeDtypeStruct((b, s, r), BF16), jax.ShapeDtypeStruct((b, n_ctx, r), BF16)],
        scratch_shapes=[pltpu.VMEM((s + 2 * SUBLANES, lanes), F32), pltpu.VMEM((n_ctx + 2 * SUBLANES, lanes), F32),
                        pltpu.VMEM((s, lanes), F32), pltpu.VMEM((n_ctx, lanes), F32)],
        compiler_params=_cparams(("parallel", "parallel")),
        name="lru",
    )(u, u, u_ctx, u_ctx, conv_w, vec(conv_b), wa, vec(ba), wx, vec(bx), vec(lam))


CONF_HALO = 16
CONF_ROWS = 64


def _conf_kernel(zp_ref, z_ref, zn_ref, x_ref, mod_ref, dw_ref, db_ref, lg_ref, lb_ref, wo_ref, bo_ref,
                 o_ref, zs_ref, y_ref, *, d, tm):
    i, n_i = pl.program_id(1), pl.num_programs(1)
    zs_ref[0:CONF_HALO] = jnp.where(i > 0, zp_ref[...], 0.0)
    zs_ref[CONF_HALO:CONF_HALO + tm] = z_ref[...]
    zs_ref[CONF_HALO + tm:2 * CONF_HALO + tm] = jnp.where(i < n_i - 1, zn_ref[...], 0.0)
    base = CONF_HALO - CONF_WIDTH // 2
    rows = min(CONF_ROWS, tm)

    def lane_chunk(c, carry):
        l0 = pl.multiple_of(c * LANES, LANES)
        w = dw_ref[:, pl.ds(l0, LANES)]
        bias = db_ref[:, pl.ds(l0, LANES)]
        for r0 in range(0, tm, rows):
            acc = jnp.broadcast_to(bias, (rows, LANES))
            for k in range(CONF_WIDTH):
                acc = acc + w[k:k + 1] * zs_ref[base + r0 + k:base + r0 + k + rows, pl.ds(l0, LANES)]
            y_ref[r0:r0 + rows, pl.ds(l0, LANES)] = acc
        return carry

    lax.fori_loop(0, d // LANES, lane_chunk, 0)
    y = y_ref[...]
    mu = jnp.mean(y, axis=-1, keepdims=True)
    yc = y - mu
    var = jnp.mean(yc * yc, axis=-1, keepdims=True)
    y = _silu(yc * lax.rsqrt(var + NORM_EPS) * lg_ref[...] + lb_ref[...])
    out = jnp.dot(y.astype(BF16), wo_ref[...], preferred_element_type=F32) + bo_ref[...]
    o_ref[...] = x_ref[...] + _mod_slice(mod_ref, 2, d) * out


def _conf_call(z, x, mod, mod_row, dw_w, dw_b, ln_g, ln_b, w_out, b_out, *, tm_pref=256, name="conf"):
    nb, s, d = x.shape
    tm = _tile(s, tm_pref)
    rh = tm // CONF_HALO
    row = lambda v: v.reshape(1, d)
    mod_map = (lambda b, i: (b, 0, 0)) if mod_row is None else (lambda b, i: (mod_row, 0, 0))
    const = lambda shape: pl.BlockSpec(shape, lambda b, i: (0, 0))
    return pl.pallas_call(
        functools.partial(_conf_kernel, d=d, tm=tm),
        grid=(nb, s // tm),
        in_specs=[pl.BlockSpec((None, CONF_HALO, d), lambda b, i: (b, jnp.maximum(i * rh - 1, 0), 0)),
                  pl.BlockSpec((None, tm, d), lambda b, i: (b, i, 0)),
                  pl.BlockSpec((None, CONF_HALO, d),
                               lambda b, i: (b, jnp.minimum((i + 1) * rh, s // CONF_HALO - 1), 0)),
                  pl.BlockSpec((None, tm, d), lambda b, i: (b, i, 0)),
                  pl.BlockSpec((None, 1, mod.shape[-1]), mod_map),
                  const((CONF_WIDTH, d)), const((1, d)), const((1, d)), const((1, d)),
                  const((d, d)), const((1, d))],
        out_specs=pl.BlockSpec((None, tm, d), lambda b, i: (b, i, 0)),
        out_shape=jax.ShapeDtypeStruct((nb, s, d), F32),
        scratch_shapes=[pltpu.VMEM((tm + 2 * CONF_HALO, d), F32), pltpu.VMEM((tm, d), F32)],
        compiler_params=_cparams(("parallel", "parallel")),
        name=name,
    )(z, z, z, x, mod, dw_w, row(dw_b), row(ln_g), row(ln_b), w_out, row(b_out))


def kernel(x, c, ctx, c_ctx, ada_w, ada_b, norm_mix_g, norm_ffn_g, attn_w_qkv, attn_w_o, attn_sink, lru_w_in, lru_conv_w, lru_conv_b, lru_wa, lru_ba, lru_wx, lru_bx, lru_lambda, lru_w_out, conf_w_in, conf_b_in, conf_dw_w, conf_dw_b, conf_ln_g, conf_ln_b, conf_w_out, conf_b_out, ffn_w_up, ffn_conv_w, ffn_conv_b, ffn_w_down, final_norm_g):
    bsz, seq, d = x.shape
    depth = ada_w.shape[0]
    assert bsz + 1 <= MOD_ROWS
    ctx_row = bsz
    bf = lambda w: w.astype(BF16)

    cvec = jnp.zeros((MOD_ROWS, d), F32).at[:bsz].set(c).at[ctx_row].set(c_ctx)
    mods = _ada_call(cvec, ada_w, ada_b)
    rope = _rope_tables(seq)
    q_cols = N_HEADS * HEAD_DIM
    k_cols = N_KV_HEADS * HEAD_DIM
    xc = ctx

    for i in range(depth):
        last = i == depth - 1
        kind, j = i % N_MIXERS, i // N_MIXERS
        mod = mods[i][:, None, :]
        g_mix = norm_mix_g[i]
        ctx_used = (not last) or kind != 2
        if kind == 0:
            w_qkv, w_o = bf(attn_w_qkv[j]), bf(attn_w_o[j])
            qkv = _proj_call(x, mod, None, g_mix, w_qkv, rope=rope, rope_cols=q_cols + k_cols, out_dtype=BF16,
                             name="qkv")
            qkv_c = _proj_call(xc, mod, ctx_row, g_mix, w_qkv, out_dtype=BF16, name="qkv_ctx")
            o = _attn_call(qkv, qkv_c, attn_sink[j])
            x = _oproj_call(o, w_o, x, mod, None, name="attn_out")
            if not last:
                o_c = _attn_ctx_call(qkv_c, attn_sink[j])
                xc = _oproj_call(o_c, w_o, xc, mod, ctx_row, name="attn_out_ctx")
        elif kind == 1:
            w_in, w_out = bf(lru_w_in[j]), bf(lru_w_out[j])
            u = _proj_call(x, mod, None, g_mix, w_in, name="lru_in")
            u_c = _proj_call(xc, mod, ctx_row, g_mix, w_in, name="lru_in_ctx")
            z, z_c = _lru_call(u, u_c, lru_conv_w[j], lru_conv_b[j], bf(lru_wa[j]), lru_ba[j], bf(lru_wx[j]),
                               lru_bx[j], lru_lambda[j])
            x = _oproj_call(z, w_out, x, mod, None, name="lru_out")
            if not last:
                xc = _oproj_call(z_c, w_out, xc, mod, ctx_row, name="lru_out_ctx")
        else:
            w_in, w_out = bf(conf_w_in[j]), bf(conf_w_out[j])
            tail = (conf_dw_w[j], conf_dw_b[j], conf_ln_g[j], conf_ln_b[j], w_out, conf_b_out[j])
            zz = _proj_call(x, mod, None, g_mix, w_in, conf_b_in[j], glu=True, name="conf_in")
            x = _conf_call(zz, x, mod, None, *tail, name="conf_tail")
            if ctx_used and not last:
                zz_c = _proj_call(xc, mod, ctx_row, g_mix, w_in, conf_b_in[j], glu=True, name="conf_in_ctx")
                xc = _conf_call(zz_c, xc, mod, ctx_row, *tail, name="conf_tail_ctx")
        ffn_w = (norm_ffn_g[i], bf(ffn_w_up[i]), ffn_conv_w[i], ffn_conv_b[i], bf(ffn_w_down[i]))
        x = _ffn_call(x, mod, None, *ffn_w, final_g=final_norm_g if last else None, name="ffn")
        if not last:
            xc = _ffn_call(xc, mod, ctx_row, *ffn_w, name="ffn_ctx")
    return x
```

```python
import functools

import jax
import jax.numpy as jnp
from jax import lax
from jax.experimental import pallas as pl
from jax.experimental.pallas import tpu as pltpu

F32 = jnp.float32
BF16 = jnp.bfloat16

HEAD_DIM = 128
N_HEADS = 16
N_KV_HEADS = 4
GQA_GROUP = N_HEADS // N_KV_HEADS
ATT_BLOCK = 128
GRID_W = 64
ROPE_THETA = 10000.0
ROPE_FREQS = HEAD_DIM // 4
LRU_BLOCK = 128
LRU_C = 8.0
LRU_CONV = 4
CONF_WIDTH = 31
FFN_CONV = 3
NORM_EPS = 1e-6
NEG_INF = -1e30
N_MIXERS = 3

LANES = 128
SUBLANES = 8
MOD_ROWS = 8
VMEM_LIMIT = 56 * 1024 * 1024


def _cparams(sem):
    return pltpu.CompilerParams(dimension_semantics=sem, vmem_limit_bytes=VMEM_LIMIT)


def _tile(n, pref):
    t = min(n, pref)
    assert n % t == 0, (n, t)
    return t


def _silu(x):
    return x * jax.nn.sigmoid(x)


def _rms_mod(x, g, shift, scale):
    ms = jnp.mean(x * x, axis=-1, keepdims=True)
    return (x * lax.rsqrt(ms + NORM_EPS) * g) * (1.0 + scale) + shift


def _mod_slice(mod_ref, k, d):
    return mod_ref[:, k * d:(k + 1) * d]


def _ada_kernel(c_ref, w_ref, b_ref, o_ref):
    s = _silu(c_ref[...]).astype(BF16)
    o_ref[...] = jnp.dot(s, w_ref[...].astype(BF16), preferred_element_type=F32) + b_ref[...]


def _ada_call(cvec, ada_w, ada_b):
    depth, d, n = ada_w.shape
    tn = _tile(n, 1024)
    return pl.pallas_call(
        _ada_kernel,
        grid=(depth, n // tn),
        in_specs=[pl.BlockSpec((MOD_ROWS, d), lambda l, j: (0, 0)),
                  pl.BlockSpec((None, d, tn), lambda l, j: (l, 0, j)),
                  pl.BlockSpec((None, 1, tn), lambda l, j: (l, 0, j))],
        out_specs=pl.BlockSpec((None, MOD_ROWS, tn), lambda l, j: (l, 0, j)),
        out_shape=jax.ShapeDtypeStruct((depth, MOD_ROWS, n), F32),
        compiler_params=_cparams(("parallel", "parallel")),
        name="ada",
    )(cvec, ada_w, ada_b.reshape(depth, 1, n))


def _rope(y, cos, sp, sm):
    return y * cos + pltpu.roll(y, 32, 1) * sp + pltpu.roll(y, HEAD_DIM - 32, 1) * sm


def _proj_kernel(*refs, d, tn, n_out, has_bias, glu, rope_cols):
    it = iter(refs)
    x_ref, mod_ref, g_ref, w_ref = (next(it) for _ in range(4))
    b_ref = next(it) if has_bias else None
    rope_refs = [next(it) for _ in range(3 if rope_cols else 0)]
    o_ref = next(it)
    h = _rms_mod(x_ref[...], g_ref[...], _mod_slice(mod_ref, 0, d), _mod_slice(mod_ref, 1, d)).astype(BF16)
    if rope_cols:
        cos, sp, sm = (r[...] for r in rope_refs)
    for c0 in range(0, n_out, tn):
        y = jnp.dot(h, w_ref[:, c0:c0 + tn], preferred_element_type=F32)
        if has_bias:
            y = y + b_ref[:, c0:c0 + tn]
        if glu:
            y2 = jnp.dot(h, w_ref[:, n_out + c0:n_out + c0 + tn], preferred_element_type=F32)
            if has_bias:
                y2 = y2 + b_ref[:, n_out + c0:n_out + c0 + tn]
            y = y * jax.nn.sigmoid(y2)
        if c0 < rope_cols:
            for c in range(0, tn, HEAD_DIM):
                o_ref[:, c0 + c:c0 + c + HEAD_DIM] = _rope(y[:, c:c + HEAD_DIM], cos, sp, sm).astype(o_ref.dtype)
        else:
            o_ref[:, c0:c0 + tn] = y.astype(o_ref.dtype)


def _proj_call(x, mod, mod_row, norm_g, w, bias=None, *, glu=False, rope=None, rope_cols=0,
               out_dtype=F32, tm_pref=512, tn_pref=512, name="proj"):
    nb, s, d = x.shape
    n_w = w.shape[1]
    n_out = n_w // 2 if glu else n_w
    tm, tn = _tile(s, tm_pref), _tile(n_out, tn_pref)
    mod_map = (lambda b, i: (b, 0, 0)) if mod_row is None else (lambda b, i: (mod_row, 0, 0))
    const = lambda shape: pl.BlockSpec(shape, lambda b, i: (0, 0), pipeline_mode=pl.Buffered(1))
    args = [x, mod, norm_g.reshape(1, d), w]
    specs = [pl.BlockSpec((None, tm, d), lambda b, i: (b, i, 0)),
             pl.BlockSpec((None, 1, mod.shape[-1]), mod_map),
             const((1, d)), const((d, n_w))]
    if bias is not None:
        args.append(bias.reshape(1, n_w))
        specs.append(const((1, n_w)))
    if rope is not None:
        assert rope_cols % tn == 0
        for t in rope:
            args.append(t)
            specs.append(pl.BlockSpec((tm, HEAD_DIM), lambda b, i: (i, 0)))
    kern = functools.partial(_proj_kernel, d=d, tn=tn, n_out=n_out, has_bias=bias is not None, glu=glu,
                             rope_cols=rope_cols if rope is not None else 0)
    return pl.pallas_call(
        kern,
        grid=(nb, s // tm),
        in_specs=specs,
        out_specs=pl.BlockSpec((None, tm, n_out), lambda b, i: (b, i, 0)),
        out_shape=jax.ShapeDtypeStruct((nb, s, n_out), out_dtype),
        compiler_params=_cparams(("parallel", "parallel")),
        name=name,
    )(*args)


def _rope_tables(s):
    t = jnp.arange(s)
    row = (t // GRID_W).astype(F32)
    col = (t % GRID_W).astype(F32)
    freq = ROPE_THETA ** (-jnp.arange(ROPE_FREQS, dtype=F32) / ROPE_FREQS)
    dd = jnp.arange(HEAD_DIM)
    axis, half, f = dd // (2 * ROPE_FREQS), (dd % (2 * ROPE_FREQS)) // ROPE_FREQS, dd % ROPE_FREQS
    pos = jnp.where(axis[None, :] == 0, row[:, None], col[:, None])
    ang = pos * freq[f][None, :]
    cos, sin = jnp.cos(ang), jnp.sin(ang)
    sp = jnp.where(half[None, :] == 1, sin, 0.0)
    sm = jnp.where(half[None, :] == 0, -sin, 0.0)
    return cos, sp, sm


def _attn_heads(q_ref, k_parts, v_parts, sink_ref, o_ref, mask):
    tq = q_ref.shape[0]
    scale = HEAD_DIM ** -0.5
    grp = lax.broadcasted_iota(jnp.int32, (GQA_GROUP * tq, 1), 0) // tq
    for kh in range(N_KV_HEADS):
        ksl = slice(kh * HEAD_DIM, (kh + 1) * HEAD_DIM)
        kcat = jnp.concatenate([r[:, ksl] for r in k_parts], axis=0)
        vcat = jnp.concatenate([r[:, ksl] for r in v_parts], axis=0)
        qg = jnp.concatenate(
            [q_ref[:, (kh * GQA_GROUP + g) * HEAD_DIM:(kh * GQA_GROUP + g + 1) * HEAD_DIM] for g in range(GQA_GROUP)],
            axis=0)
        s = lax.dot_general(qg, kcat, (((1,), (1,)), ((), ())), preferred_element_type=F32) * scale
        if mask is not None:
            s = jnp.where(mask, s, NEG_INF)
        sink = jnp.full((GQA_GROUP * tq, 1), sink_ref[kh * GQA_GROUP], F32)
        for g in range(1, GQA_GROUP):
            sink = jnp.where(grp == g, sink_ref[kh * GQA_GROUP + g], sink)
        m = jnp.maximum(jnp.max(s, axis=-1, keepdims=True), sink)
        p = jnp.exp(s - m)
        l = jnp.sum(p, axis=-1, keepdims=True) + jnp.exp(sink - m)
        o = jnp.dot(p.astype(BF16), vcat, preferred_element_type=F32) / l
        for g in range(GQA_GROUP):
            h = kh * GQA_GROUP + g
            o_ref[:, h * HEAD_DIM:(h + 1) * HEAD_DIM] = o[g * tq:(g + 1) * tq].astype(o_ref.dtype)


def _attn_kernel(sink_ref, q_ref, kp_ref, kc_ref, kn_ref, vp_ref, vc_ref, vn_ref, kx_ref, vx_ref, o_ref):
    n = pl.program_id(1)
    nblk = pl.num_programs(1)
    tq = ATT_BLOCK
    n_loc, n_ctx = 3 * tq, kx_ref.shape[0]
    r = lax.broadcasted_iota(jnp.int32, (GQA_GROUP * tq, n_loc + n_ctx), 0) % tq
    col = lax.broadcasted_iota(jnp.int32, (GQA_GROUP * tq, n_loc + n_ctx), 1)
    lo = jnp.where(n == 0, tq, 0)
    hi = jnp.where(n == nblk - 1, 2 * tq, n_loc)
    local_ok = (col >= jnp.maximum(r, lo)) & (col <= r + 2 * tq) & (col < hi)
    mask = local_ok | (col >= n_loc)
    _attn_heads(q_ref, [kp_ref, kc_ref, kn_ref, kx_ref], [vp_ref, vc_ref, vn_ref, vx_ref], sink_ref, o_ref, mask)


def _attn_call(qkv, qkv_ctx, sink):
    b, s, _ = qkv.shape
    n_ctx = qkv_ctx.shape[1]
    tq = ATT_BLOCK
    nblk = s // tq
    qc, kc = N_HEADS * HEAD_DIM, N_KV_HEADS * HEAD_DIM
    kblk, vblk = qc // kc, qc // kc + 1
    kv_spec = lambda off, cb: pl.BlockSpec(
        (None, tq, kc), lambda bb, n: (bb, jnp.clip(n + off, 0, nblk - 1), cb))
    return pl.pallas_call(
        _attn_kernel,
        grid=(b, nblk),
        in_specs=[pl.BlockSpec(memory_space=pltpu.SMEM),
                  pl.BlockSpec((None, tq, qc), lambda bb, n: (bb, n, 0)),
                  kv_spec(-1, kblk), kv_spec(0, kblk), kv_spec(1, kblk),
                  kv_spec(-1, vblk), kv_spec(0, vblk), kv_spec(1, vblk),
                  pl.BlockSpec((None, n_ctx, kc), lambda bb, n: (bb, 0, kblk)),
                  pl.BlockSpec((None, n_ctx, kc), lambda bb, n: (bb, 0, vblk))],
        out_specs=pl.BlockSpec((None, tq, qc), lambda bb, n: (bb, n, 0)),
        out_shape=jax.ShapeDtypeStruct((b, s, qc), BF16),
        compiler_params=_cparams(("parallel", "parallel")),
        name="attn",
    )(sink, qkv, qkv, qkv, qkv, qkv, qkv, qkv, qkv_ctx, qkv_ctx)


def _attn_ctx_kernel(sink_ref, q_ref, kx_ref, vx_ref, o_ref):
    _attn_heads(q_ref, [kx_ref], [vx_ref], sink_ref, o_ref, None)


def _attn_ctx_call(qkv_ctx, sink):
    b, n_ctx, _ = qkv_ctx.shape
    qc, kc = N_HEADS * HEAD_DIM, N_KV_HEADS * HEAD_DIM
    tq = _tile(n_ctx, ATT_BLOCK)
    return pl.pallas_call(
        _attn_ctx_kernel,
        grid=(b, n_ctx // tq),
        in_specs=[pl.BlockSpec(memory_space=pltpu.SMEM),
                  pl.BlockSpec((None, tq, qc), lambda bb, n: (bb, n, 0)),
                  pl.BlockSpec((None, n_ctx, kc), lambda bb, n: (bb, 0, qc // kc)),
                  pl.BlockSpec((None, n_ctx, kc), lambda bb, n: (bb, 0, qc // kc + 1))],
        out_specs=pl.BlockSpec((None, tq, qc), lambda bb, n: (bb, n, 0)),
        out_shape=jax.ShapeDtypeStruct((b, n_ctx, qc), BF16),
        compiler_params=_cparams(("parallel", "parallel")),
        name="attn_ctx",
    )(sink, qkv_ctx, qkv_ctx, qkv_ctx)


def _oproj_kernel(a_ref, w_ref, x_ref, mod_ref, o_ref, *, d, gate_idx):
    y = jnp.dot(a_ref[...], w_ref[...], preferred_element_type=F32)
    o_ref[...] = x_ref[...] + _mod_slice(mod_ref, gate_idx, d) * y


def _oproj_call(a, w, x, mod, mod_row, *, gate_idx=2, tm_pref=512, name="oproj"):
    nb, s, d = x.shape
    k = a.shape[-1]
    tm = _tile(s, tm_pref)
    mod_map = (lambda b, i: (b, 0, 0)) if mod_row is None else (lambda b, i: (mod_row, 0, 0))
    return pl.pallas_call(
        functools.partial(_oproj_kernel, d=d, gate_idx=gate_idx),
        grid=(nb, s // tm),
        in_specs=[pl.BlockSpec((None, tm, k), lambda b, i: (b, i, 0)),
                  pl.BlockSpec((k, d), lambda b, i: (0, 0)),
                  pl.BlockSpec((None, tm, d), lambda b, i: (b, i, 0)),
                  pl.BlockSpec((None, 1, mod.shape[-1]), mod_map)],
        out_specs=pl.BlockSpec((None, tm, d), lambda b, i: (b, i, 0)),
        out_shape=jax.ShapeDtypeStruct((nb, s, d), F32),
        compiler_params=_cparams(("parallel", "parallel")),
        name=name,
    )(a, w, x, mod)


def _ffn_kernel(*refs, d, tm, final_norm):
    it = iter(refs)
    xp_ref, x_ref, xn_ref, mod_ref, g_ref = (next(it) for _ in range(5))
    wg_ref, wv_ref, cw_ref, cb_ref, wd_ref = (next(it) for _ in range(5))
    fg_ref = next(it) if final_norm else None
    o_ref, h_ref, acc_ref = next(it), next(it), next(it)
    i, f = pl.program_id(1), pl.program_id(2)
    n_i, n_f = pl.num_programs(1), pl.num_programs(2)
    halo = SUBLANES

    @pl.when(f == 0)
    def _():
        xs = jnp.concatenate([xp_ref[...], x_ref[...], xn_ref[...]], axis=0)
        h_ref[...] = _rms_mod(xs, g_ref[...], _mod_slice(mod_ref, 3, d), _mod_slice(mod_ref, 4, d)).astype(BF16)
        acc_ref[...] = jnp.zeros_like(acc_ref)

    h = h_ref[...]
    gate = jnp.dot(h, wg_ref[...], preferred_element_type=F32)
    val = jnp.dot(h, wv_ref[...], preferred_element_type=F32)[halo:halo + tm]
    row = lax.broadcasted_iota(jnp.int32, (tm + 2 * halo, 1), 0)
    keep = ((row >= halo) | (i > 0)) & ((row < tm + halo) | (i < n_i - 1))
    gate = jnp.where(keep, gate, 0.0)
    cw = cw_ref[...]
    gc = cb_ref[...]
    for k in range(FFN_CONV):
        off = halo + k - FFN_CONV // 2
        gc = gc + cw[k:k + 1] * gate[off:off + tm]
    act = (_silu(gc) * val).astype(BF16)
    part = jnp.dot(act, wd_ref[...], preferred_element_type=F32)

    acc_ref[...] += part

    @pl.when(f == n_f - 1)
    def _():
        out = x_ref[...] + _mod_slice(mod_ref, 5, d) * acc_ref[...]
        if final_norm:
            ms = jnp.mean(out * out, axis=-1, keepdims=True)
            out = out * lax.rsqrt(ms + NORM_EPS) * fg_ref[...]
        o_ref[...] = out


def _ffn_call(x, mod, mod_row, norm_g, w_up, conv_w, conv_b, w_down, final_g=None, *, tm_pref=512, tf_pref=512,
              name="ffn"):
    nb, s, d = x.shape
    ff = w_down.shape[0]
    tm, tf = _tile(s, tm_pref), _tile(ff, tf_pref)
    n_i, n_f = s // tm, ff // tf
    r8 = tm // SUBLANES
    mod_map = (lambda b, i, f: (b, 0, 0)) if mod_row is None else (lambda b, i, f: (mod_row, 0, 0))
    args = [x, x, x, mod, norm_g.reshape(1, d), w_up, w_up, conv_w, conv_b.reshape(1, ff), w_down]
    specs = [pl.BlockSpec((None, SUBLANES, d), lambda b, i, f: (b, jnp.maximum(i * r8 - 1, 0), 0)),
             pl.BlockSpec((None, tm, d), lambda b, i, f: (b, i, 0)),
             pl.BlockSpec((None, SUBLANES, d), lambda b, i, f: (b, jnp.minimum((i + 1) * r8, s // SUBLANES - 1), 0)),
             pl.BlockSpec((None, 1, mod.shape[-1]), mod_map),
             pl.BlockSpec((1, d), lambda b, i, f: (0, 0)),
             pl.BlockSpec((d, tf), lambda b, i, f: (0, f)),
             pl.BlockSpec((d, tf), lambda b, i, f: (0, f + n_f)),
             pl.BlockSpec((FFN_CONV, tf), lambda b, i, f: (0, f)),
             pl.BlockSpec((1, tf), lambda b, i, f: (0, f)),
             pl.BlockSpec((tf, d), lambda b, i, f: (f, 0))]
    if final_g is not None:
        args.append(final_g.reshape(1, d))
        specs.append(pl.BlockSpec((1, d), lambda b, i, f: (0, 0)))
    return pl.pallas_call(
        functools.partial(_ffn_kernel, d=d, tm=tm, final_norm=final_g is not None),
        grid=(nb, n_i, n_f),
        in_specs=specs,
        out_specs=pl.BlockSpec((None, tm, d), lambda b, i, f: (b, i, 0)),
        out_shape=jax.ShapeDtypeStruct((nb, s, d), F32),
        scratch_shapes=[pltpu.VMEM((tm + 2 * SUBLANES, d), BF16), pltpu.VMEM((tm, d), F32)],
        compiler_params=_cparams(("parallel", "parallel", "arbitrary")),
        name=name,
    )(*args)


LRU_SEG = SUBLANES
LRU_CH = 32


def _lru_pitch(n):
    p = n // LRU_SEG + SUBLANES
    return p if (p // SUBLANES) % 2 == 1 else p + SUBLANES


def _fill_pitched(xp_ref, x_ref, n):
    seg, pitch = n // LRU_SEG, _lru_pitch(n)
    lanes = x_ref.shape[1]
    zero = jnp.zeros((SUBLANES, lanes), F32)
    row = lax.broadcasted_iota(jnp.int32, (SUBLANES, lanes), 0)
    xp_ref[0:SUBLANES] = zero
    for j in range(LRU_SEG):
        base = SUBLANES + j * pitch
        xp_ref[base:base + seg] = x_ref[j * seg:(j + 1) * seg]
        tail = x_ref[(j + 1) * seg - SUBLANES:(j + 1) * seg]
        head = x_ref[(j + 1) * seg:(j + 1) * seg + SUBLANES] if j + 1 < LRU_SEG else zero
        if pitch == seg + SUBLANES:
            xp_ref[base + seg:base + pitch] = jnp.where(row < SUBLANES // 2, head, tail)
        else:
            xp_ref[base + seg:base + seg + SUBLANES] = head
            xp_ref[base + pitch - SUBLANES:base + pitch] = tail


def _seg_rows(step, pitch):
    return pl.ds(SUBLANES + step, LRU_SEG, stride=pitch)


def _lru_chunk(xp_ref, hs_ref, ac_ref, pitch, s0, rev, par, h, ac):
    cwb, cbb, wa, ba, wx, bx, kq = par
    lo = 0 if rev else -(LRU_CONV - 1)
    xs = [xp_ref[_seg_rows(s0 + lo + i, pitch), :] for i in range(LRU_CH + LRU_CONV - 1)]
    ucs = []
    for s in range(LRU_CH):
        u = cbb
        for k in range(LRU_CONV):
            u = u + cwb[k] * xs[s + k]
        ucs.append(u)
    uc = jnp.concatenate(ucs, axis=0)
    ucb = uc.astype(BF16)
    ta = jnp.tanh(0.5 * (jnp.dot(ucb, wa, preferred_element_type=F32) + ba))
    tx = jnp.tanh(0.5 * (jnp.dot(ucb, wx, preferred_element_type=F32) + bx))
    t = jnp.tanh(kq + kq * ta)
    q = 1.0 / (1.0 - t)
    a = (1.0 + t) * q
    bt = (2.0 * jnp.sqrt(-t) * q) * ((0.5 + 0.5 * tx) * uc)
    hs, acs = [None] * LRU_CH, [None] * LRU_CH
    for s in (range(LRU_CH - 1, -1, -1) if rev else range(LRU_CH)):
        a_s = a[s * LRU_SEG:(s + 1) * LRU_SEG]
        h = a_s * h + bt[s * LRU_SEG:(s + 1) * LRU_SEG]
        ac = a_s * ac
        hs[s], acs[s] = h, ac
    r0 = pl.multiple_of(s0 * LRU_SEG, LRU_CH * LRU_SEG)
    hs_ref[pl.ds(r0, LRU_CH * LRU_SEG), :] = jnp.concatenate(hs, axis=0)
    ac_ref[pl.ds(r0, LRU_CH * LRU_SEG), :] = jnp.concatenate(acs, axis=0)
    return h, ac


def _lru_pass(xp_ref, scr, n, pars):
    seg, pitch = n // LRU_SEG, _lru_pitch(n)
    nch = seg // LRU_CH
    lanes = xp_ref.shape[1]

    def body(ci, carry):
        hf, af, hr, ar = carry
        hf, af = _lru_chunk(xp_ref, scr[0], scr[1], pitch, ci * LRU_CH, False, pars[0], hf, af)
        hr, ar = _lru_chunk(xp_ref, scr[2], scr[3], pitch, (nch - 1 - ci) * LRU_CH, True, pars[1], hr, ar)
        return hf, af, hr, ar

    zero, one = jnp.zeros((LRU_SEG, lanes), F32), jnp.ones((LRU_SEG, lanes), F32)
    hf, af, hr, ar = lax.fori_loop(0, nch, body, (zero, one, zero, one))
    return (hf, af), (hr, ar)


def _segment_carries(h_end, a_end, h_in, rev):
    row = lax.broadcasted_iota(jnp.int32, h_end.shape, 0)
    sh = LRU_SEG - 1 if rev else 1
    ph, pa = pltpu.roll(h_end, sh, 0), pltpu.roll(a_end, sh, 0)
    c = h_in
    for j in (range(LRU_SEG - 2, -1, -1) if rev else range(1, LRU_SEG)):
        c = jnp.where(row == j, ph + pa * pltpu.roll(c, sh, 0), c)
    last = 0 if rev else LRU_SEG - 1
    end = (h_end + a_end * c)[last:last + 1]
    return c, jnp.broadcast_to(end, h_end.shape)


def _lru_finish(hs_f, ac_f, hs_r, ac_r, c_f, c_r, xp_ref, g_ref, z_ref, n):
    seg, pitch = n // LRU_SEG, _lru_pitch(n)
    nch = seg // LRU_CH
    rows = LRU_CH * LRU_SEG
    cf = jnp.concatenate([c_f] * LRU_CH, axis=0)
    cr = jnp.concatenate([c_r] * LRU_CH, axis=0)

    def comb(ci, carry):
        r0 = pl.multiple_of(ci * rows, rows)
        sl = pl.ds(r0, rows)
        tot = (hs_f[sl, :] + ac_f[sl, :] * cf) + (hs_r[sl, :] + ac_r[sl, :] * cr)
        for s in range(LRU_CH):
            xp_ref[_seg_rows(ci * LRU_CH + s, pitch), :] = tot[s * LRU_SEG:(s + 1) * LRU_SEG]
        return carry

    lax.fori_loop(0, nch, comb, 0)
    rt = min(seg, 128)
    for j in range(LRU_SEG):
        def gate(ri, carry, j=j):
            r = pl.multiple_of(ri * rt, rt)
            tot = xp_ref[pl.ds(SUBLANES + j * pitch + r, rt), :]
            z_ref[pl.ds(j * seg + r, rt), :] = (jax.nn.gelu(g_ref[pl.ds(j * seg + r, rt), :]) * tot).astype(z_ref.dtype)
            return carry

        lax.fori_loop(0, seg // rt, gate, 0)


def _lru_kernel(gl_ref, xl_ref, gc_ref, xc_ref, cw_ref, cb_ref, wa_ref, ba_ref, wx_ref, bx_ref, lam_ref,
                zl_ref, zc_ref, xpl_ref, xpc_ref, *scr):
    s, n_ctx = xl_ref.shape[0], xc_ref.shape[0]
    lanes = xl_ref.shape[1]
    lat, ctx = scr[:4], scr[4:]
    _fill_pitched(xpc_ref, xc_ref, n_ctx)
    _fill_pitched(xpl_ref, xl_ref, s)
    pars = []
    for dr in range(2):
        nl = -lam_ref[dr]
        sp = jnp.maximum(nl, 0.0) + jnp.log1p(jnp.exp(-jnp.abs(nl)))
        cw = cw_ref[dr]
        cwb = [jnp.broadcast_to(cw[k:k + 1], (LRU_SEG, lanes)) for k in range(LRU_CONV)]
        pars.append((cwb, jnp.broadcast_to(cb_ref[dr], (LRU_SEG, lanes)), wa_ref[dr], ba_ref[dr], wx_ref[dr],
                     bx_ref[dr], (-0.25 * LRU_C) * sp))
    zero = jnp.zeros((LRU_SEG, lanes), F32)
    ends_c = _lru_pass(xpc_ref, ctx, n_ctx, pars)
    ends_l = _lru_pass(xpl_ref, lat, s, pars)
    c_ctx, c_lat = [], []
    for dr in range(2):
        c, state = _segment_carries(*ends_c[dr], zero, dr == 1)
        c_ctx.append(c)
        c_lat.append(_segment_carries(*ends_l[dr], state, dr == 1)[0])
    _lru_finish(*ctx, *c_ctx, xpc_ref, gc_ref, zc_ref, n_ctx)
    _lru_finish(*lat, *c_lat, xpl_ref, gl_ref, zl_ref, s)


def _lru_call(u, u_ctx, conv_w, conv_b, wa, ba, wx, bx, lam):
    b, s, r2 = u.shape
    r = r2 // 2
    n_ctx = u_ctx.shape[1]
    lanes = LRU_BLOCK
    nblk = r // lanes
    for n in (s, n_ctx):
        assert n % (LRU_SEG * LRU_CH) == 0, n
    vec = lambda v: v.reshape(2, 1, r)
    col = lambda n, off: pl.BlockSpec((None, n, lanes), lambda bb, c: (bb, 0, c + off))
    par = lambda rows: pl.BlockSpec((2, rows, lanes), lambda bb, c: (0, 0, c))
    wsp = pl.BlockSpec((2, None, lanes, lanes), lambda bb, c: (0, c, 0, 0))
    pitched = lambda n: pltpu.VMEM((SUBLANES + LRU_SEG * _lru_pitch(n), lanes), F32)
    return pl.pallas_call(
        _lru_kernel,
        grid=(b, nblk),
        in_specs=[col(s, 0), col(s, nblk), col(n_ctx, 0), col(n_ctx, nblk),
                  par(LRU_CONV), par(1), wsp, par(1), wsp, par(1), par(1)],
        out_specs=[col(s, 0), col(n_ctx, 0)],
        out_shape=[jax.ShapeDtypeStruct((b, s, r), BF16), jax.ShapeDtypeStruct((b, n_ctx, r), BF16)],
        scratch_shapes=[pitched(s), pitched(n_ctx)] + [pltpu.VMEM((s, lanes), F32)] * 4
                       + [pltpu.VMEM((n_ctx, lanes), F32)] * 4,
        compiler_params=_cparams(("parallel", "parallel")),
        name="lru",
    )(u, u, u_ctx, u_ctx, conv_w, vec(conv_b), wa, vec(ba), wx, vec(bx), vec(lam))


CONF_HALO = 16
CONF_ROWS = 64
CONF_PARTS = 2


def _conf_kernel(zp_ref, z_ref, zn_ref, x_ref, mod_ref, dw_ref, db_ref, lg_ref, lb_ref, wo_ref, bo_ref,
                 o_ref, zw_ref, y_ref, *, d, tm):
    i, n_i = pl.program_id(1), pl.num_programs(1)
    n_slab = d // LANES
    zprev = jnp.where(i > 0, zp_ref[...], 0.0)
    znext = jnp.where(i < n_i - 1, zn_ref[...], 0.0)
    for c in range(n_slab):
        sl = slice(c * LANES, (c + 1) * LANES)
        zw_ref[c, 0:CONF_HALO] = zprev[:, sl]
        zw_ref[c, CONF_HALO:CONF_HALO + tm] = z_ref[:, sl]
        zw_ref[c, CONF_HALO + tm:2 * CONF_HALO + tm] = znext[:, sl]
    base = CONF_HALO - CONF_WIDTH // 2
    rows = min(CONF_ROWS, tm)
    pair = 2 * SUBLANES
    n_ld = rows + pair

    def slab(c, carry):
        l0 = pl.multiple_of(c * LANES, LANES)
        w = dw_ref[:, pl.ds(l0, LANES)]
        wb = [jnp.broadcast_to(w[k:k + 1], (SUBLANES, LANES)) for k in range(CONF_WIDTH)]
        bias = jnp.broadcast_to(db_ref[:, pl.ds(l0, LANES)], (SUBLANES, LANES))

        def block(rb, carry2):
            r0 = rb * rows
            z2 = [zw_ref[c, pl.ds(r0 + base + a, SUBLANES, stride=2), :] for a in range(n_ld)]
            for p in range(rows // pair):
                for odd in range(2):
                    parts = [None] * CONF_PARTS
                    for k in range(CONF_WIDTH):
                        term = wb[k] * z2[p * pair + odd + k]
                        q = k % CONF_PARTS
                        parts[q] = term if parts[q] is None else parts[q] + term
                    acc = bias
                    for q in range(CONF_PARTS):
                        acc = acc + parts[q]
                    y_ref[c, pl.ds(r0 + p * pair + odd, SUBLANES, stride=2), :] = acc
            return carry2

        lax.fori_loop(0, tm // rows, block, 0)
        return carry

    lax.fori_loop(0, n_slab, slab, 0)
    y = jnp.concatenate([y_ref[c] for c in range(n_slab)], axis=1)
    mu = jnp.mean(y, axis=-1, keepdims=True)
    yc = y - mu
    var = jnp.mean(yc * yc, axis=-1, keepdims=True)
    y = _silu(yc * lax.rsqrt(var + NORM_EPS) * lg_ref[...] + lb_ref[...])
    out = jnp.dot(y.astype(BF16), wo_ref[...], preferred_element_type=F32) + bo_ref[...]
    o_ref[...] = x_ref[...] + _mod_slice(mod_ref, 2, d) * out


def _conf_call(z, x, mod, mod_row, dw_w, dw_b, ln_g, ln_b, w_out, b_out, *, tm_pref=256, name="conf"):
    nb, s, d = x.shape
    tm = _tile(s, tm_pref)
    rh = tm // CONF_HALO
    row = lambda v: v.reshape(1, d)
    mod_map = (lambda b, i: (b, 0, 0)) if mod_row is None else (lambda b, i: (mod_row, 0, 0))
    const = lambda shape: pl.BlockSpec(shape, lambda b, i: (0, 0), pipeline_mode=pl.Buffered(1))
    return pl.pallas_call(
        functools.partial(_conf_kernel, d=d, tm=tm),
        grid=(nb, s // tm),
        in_specs=[pl.BlockSpec((None, CONF_HALO, d), lambda b, i: (b, jnp.maximum(i * rh - 1, 0), 0)),
                  pl.BlockSpec((None, tm, d), lambda b, i: (b, i, 0)),
                  pl.BlockSpec((None, CONF_HALO, d),
                               lambda b, i: (b, jnp.minimum((i + 1) * rh, s // CONF_HALO - 1), 0)),
                  pl.BlockSpec((None, tm, d), lambda b, i: (b, i, 0)),
                  pl.BlockSpec((None, 1, mod.shape[-1]), mod_map),
                  const((CONF_WIDTH, d)), const((1, d)), const((1, d)), const((1, d)),
                  const((d, d)), const((1, d))],
        out_specs=pl.BlockSpec((None, tm, d), lambda b, i: (b, i, 0)),
        out_shape=jax.ShapeDtypeStruct((nb, s, d), F32),
        scratch_shapes=[pltpu.VMEM((d // LANES, tm + 2 * CONF_HALO, LANES), F32),
                        pltpu.VMEM((d // LANES, tm, LANES), F32)],
        compiler_params=_cparams(("parallel", "parallel")),
        name=name,
    )(z, z, z, x, mod, dw_w, row(dw_b), row(ln_g), row(ln_b), w_out, row(b_out))


def kernel(x, c, ctx, c_ctx, ada_w, ada_b, norm_mix_g, norm_ffn_g, attn_w_qkv, attn_w_o, attn_sink, lru_w_in, lru_conv_w, lru_conv_b, lru_wa, lru_ba, lru_wx, lru_bx, lru_lambda, lru_w_out, conf_w_in, conf_b_in, conf_dw_w, conf_dw_b, conf_ln_g, conf_ln_b, conf_w_out, conf_b_out, ffn_w_up, ffn_conv_w, ffn_conv_b, ffn_w_down, final_norm_g):
    bsz, seq, d = x.shape
    depth = ada_w.shape[0]
    assert bsz + 1 <= MOD_ROWS
    ctx_row = bsz
    bf = lambda w: w.astype(BF16)

    cvec = jnp.zeros((MOD_ROWS, d), F32).at[:bsz].set(c).at[ctx_row].set(c_ctx)
    mods = _ada_call(cvec, ada_w, ada_b)
    rope = _rope_tables(seq)
    q_cols = N_HEADS * HEAD_DIM
    k_cols = N_KV_HEADS * HEAD_DIM
    xc = ctx

    for i in range(depth):
        last = i == depth - 1
        kind, j = i % N_MIXERS, i // N_MIXERS
        mod = mods[i][:, None, :]
        g_mix = norm_mix_g[i]
        ctx_used = (not last) or kind != 2
        if kind == 0:
            w_qkv, w_o = bf(attn_w_qkv[j]), bf(attn_w_o[j])
            qkv = _proj_call(x, mod, None, g_mix, w_qkv, rope=rope, rope_cols=q_cols + k_cols, out_dtype=BF16,
                             name="qkv")
            qkv_c = _proj_call(xc, mod, ctx_row, g_mix, w_qkv, out_dtype=BF16, name="qkv_ctx")
            o = _attn_call(qkv, qkv_c, attn_sink[j])
            x = _oproj_call(o, w_o, x, mod, None, name="attn_out")
            if not last:
                o_c = _attn_ctx_call(qkv_c, attn_sink[j])
                xc = _oproj_call(o_c, w_o, xc, mod, ctx_row, name="attn_out_ctx")
        elif kind == 1:
            w_in, w_out = bf(lru_w_in[j]), bf(lru_w_out[j])
            u = _proj_call(x, mod, None, g_mix, w_in, name="lru_in")
            u_c = _proj_call(xc, mod, ctx_row, g_mix, w_in, name="lru_in_ctx")
            z, z_c = _lru_call(u, u_c, lru_conv_w[j], lru_conv_b[j], bf(lru_wa[j]), lru_ba[j], bf(lru_wx[j]),
                               lru_bx[j], lru_lambda[j])
            x = _oproj_call(z, w_out, x, mod, None, name="lru_out")
            if not last:
                xc = _oproj_call(z_c, w_out, xc, mod, ctx_row, name="lru_out_ctx")
        else:
            w_in, w_out = bf(conf_w_in[j]), bf(conf_w_out[j])
            tail = (conf_dw_w[j], conf_dw_b[j], conf_ln_g[j], conf_ln_b[j], w_out, conf_b_out[j])
            zz = _proj_call(x, mod, None, g_mix, w_in, conf_b_in[j], glu=True, name="conf_in")
            x = _conf_call(zz, x, mod, None, *tail, name="conf_tail")
            if ctx_used and not last:
                zz_c = _proj_call(xc, mod, ctx_row, g_mix, w_in, conf_b_in[j], glu=True, name="conf_in_ctx")
                xc = _conf_call(zz_c, xc, mod, ctx_row, *tail, name="conf_tail_ctx")
        ffn_w = (norm_ffn_g[i], bf(ffn_w_up[i]), ffn_conv_w[i], ffn_conv_b[i], bf(ffn_w_down[i]))
        x = _ffn_call(x, mod, None, *ffn_w, final_g=final_norm_g if last else None, name="ffn")
        if not last:
            xc = _ffn_call(xc, mod, ctx_row, *ffn_w, name="ffn_ctx")
    return x
```

```python
import functools

import jax
import jax.numpy as jnp
from jax import lax
from jax.experimental import pallas as pl
from jax.experimental.pallas import tpu as pltpu

F32 = jnp.float32
BF16 = jnp.bfloat16

HEAD_DIM = 128
N_HEADS = 16
N_KV_HEADS = 4
GQA_GROUP = N_HEADS // N_KV_HEADS
ATT_BLOCK = 128
GRID_W = 64
ROPE_THETA = 10000.0
ROPE_FREQS = HEAD_DIM // 4
LRU_BLOCK = 128
LRU_C = 8.0
LRU_CONV = 4
CONF_WIDTH = 31
FFN_CONV = 3
NORM_EPS = 1e-6
NEG_INF = -1e30
N_MIXERS = 3

LANES = 128
SUBLANES = 8
MOD_ROWS = 8
VMEM_LIMIT = 56 * 1024 * 1024


def _cparams(sem):
    return pltpu.CompilerParams(dimension_semantics=sem, vmem_limit_bytes=VMEM_LIMIT)


def _tile(n, pref):
    t = min(n, pref)
    assert n % t == 0, (n, t)
    return t


def _silu(x):
    return x * jax.nn.sigmoid(x)


def _rms_mod(x, g, shift, scale):
    ms = jnp.mean(x * x, axis=-1, keepdims=True)
    return (x * lax.rsqrt(ms + NORM_EPS) * g) * (1.0 + scale) + shift


def _mod_slice(mod_ref, k, d):
    return mod_ref[:, k * d:(k + 1) * d]


def _ada_kernel(c_ref, w_ref, b_ref, o_ref):
    s = _silu(c_ref[...]).astype(BF16)
    o_ref[...] = jnp.dot(s, w_ref[...].astype(BF16), preferred_element_type=F32) + b_ref[...]


def _ada_call(cvec, ada_w, ada_b):
    depth, d, n = ada_w.shape
    tn = _tile(n, 1024)
    return pl.pallas_call(
        _ada_kernel,
        grid=(depth, n // tn),
        in_specs=[pl.BlockSpec((MOD_ROWS, d), lambda l, j: (0, 0)),
                  pl.BlockSpec((None, d, tn), lambda l, j: (l, 0, j)),
                  pl.BlockSpec((None, 1, tn), lambda l, j: (l, 0, j))],
        out_specs=pl.BlockSpec((None, MOD_ROWS, tn), lambda l, j: (l, 0, j)),
        out_shape=jax.ShapeDtypeStruct((depth, MOD_ROWS, n), F32),
        compiler_params=_cparams(("parallel", "parallel")),
        name="ada",
    )(cvec, ada_w, ada_b.reshape(depth, 1, n))


def _rope(y, cos, sp, sm):
    return y * cos + pltpu.roll(y, 32, 1) * sp + pltpu.roll(y, HEAD_DIM - 32, 1) * sm


def _proj_kernel(*refs, d, tn, n_out, has_bias, glu, rope_cols):
    it = iter(refs)
    x_ref, mod_ref, g_ref, w_ref = (next(it) for _ in range(4))
    b_ref = next(it) if has_bias else None
    rope_refs = [next(it) for _ in range(3 if rope_cols else 0)]
    o_ref = next(it)
    h = _rms_mod(x_ref[...], g_ref[...], _mod_slice(mod_ref, 0, d), _mod_slice(mod_ref, 1, d)).astype(BF16)
    if rope_cols:
        cos, sp, sm = (r[...] for r in rope_refs)
    for c0 in range(0, n_out, tn):
        y = jnp.dot(h, w_ref[:, c0:c0 + tn], preferred_element_type=F32)
        if has_bias:
            y = y + b_ref[:, c0:c0 + tn]
        if glu:
            y2 = jnp.dot(h, w_ref[:, n_out + c0:n_out + c0 + tn], preferred_element_type=F32)
            if has_bias:
                y2 = y2 + b_ref[:, n_out + c0:n_out + c0 + tn]
            y = y * jax.nn.sigmoid(y2)
        if c0 < rope_cols:
            for c in range(0, tn, HEAD_DIM):
                o_ref[:, c0 + c:c0 + c + HEAD_DIM] = _rope(y[:, c:c + HEAD_DIM], cos, sp, sm).astype(o_ref.dtype)
        else:
            o_ref[:, c0:c0 + tn] = y.astype(o_ref.dtype)


def _proj_call(x, mod, mod_row, norm_g, w, bias=None, *, glu=False, rope=None, rope_cols=0,
               out_dtype=F32, tm_pref=512, tn_pref=512, name="proj"):
    nb, s, d = x.shape
    n_w = w.shape[1]
    n_out = n_w // 2 if glu else n_w
    tm, tn = _tile(s, tm_pref), _tile(n_out, tn_pref)
    mod_map = (lambda b, i: (b, 0, 0)) if mod_row is None else (lambda b, i: (mod_row, 0, 0))
    const = lambda shape: pl.BlockSpec(shape, lambda b, i: (0, 0), pipeline_mode=pl.Buffered(1))
    args = [x, mod, norm_g.reshape(1, d), w]
    specs = [pl.BlockSpec((None, tm, d), lambda b, i: (b, i, 0)),
             pl.BlockSpec((None, 1, mod.shape[-1]), mod_map),
             const((1, d)), const((d, n_w))]
    if bias is not None:
        args.append(bias.reshape(1, n_w))
        specs.append(const((1, n_w)))
    if rope is not None:
        assert rope_cols % tn == 0
        for t in rope:
            args.append(t)
            specs.append(pl.BlockSpec((tm, HEAD_DIM), lambda b, i: (i, 0)))
    kern = functools.partial(_proj_kernel, d=d, tn=tn, n_out=n_out, has_bias=bias is not None, glu=glu,
                             rope_cols=rope_cols if rope is not None else 0)
    return pl.pallas_call(
        kern,
        grid=(nb, s // tm),
        in_specs=specs,
        out_specs=pl.BlockSpec((None, tm, n_out), lambda b, i: (b, i, 0)),
        out_shape=jax.ShapeDtypeStruct((nb, s, n_out), out_dtype),
        compiler_params=_cparams(("parallel", "parallel")),
        name=name,
    )(*args)


def _rope_tables(s):
    t = jnp.arange(s)
    row = (t // GRID_W).astype(F32)
    col = (t % GRID_W).astype(F32)
    freq = ROPE_THETA ** (-jnp.arange(ROPE_FREQS, dtype=F32) / ROPE_FREQS)
    dd = jnp.arange(HEAD_DIM)
    axis, half, f = dd // (2 * ROPE_FREQS), (dd % (2 * ROPE_FREQS)) // ROPE_FREQS, dd % ROPE_FREQS
    pos = jnp.where(axis[None, :] == 0, row[:, None], col[:, None])
    ang = pos * freq[f][None, :]
    cos, sin = jnp.cos(ang), jnp.sin(ang)
    sp = jnp.where(half[None, :] == 1, sin, 0.0)
    sm = jnp.where(half[None, :] == 0, -sin, 0.0)
    return cos, sp, sm


def _fold_lanes(parts, op):
    tiles = [p[:, j:j + LANES] for p in parts for j in range(0, p.shape[1], LANES)]
    acc = tiles[0]
    for t in tiles[1:]:
        acc = op(acc, t)
    return acc


def _attn_heads(q_ref, k_parts, v_parts, sink_ref, o_ref, masks):
    tq = q_ref.shape[0]
    log2e = 1.4426950408889634
    c = (HEAD_DIM ** -0.5) * log2e
    grp = lax.broadcasted_iota(jnp.int32, (GQA_GROUP * tq, 1), 0) // tq
    for kh in range(N_KV_HEADS):
        ksl = slice(kh * HEAD_DIM, (kh + 1) * HEAD_DIM)
        qg = jnp.concatenate(
            [q_ref[:, (kh * GQA_GROUP + g) * HEAD_DIM:(kh * GQA_GROUP + g + 1) * HEAD_DIM] for g in range(GQA_GROUP)],
            axis=0)
        sink = jnp.full((GQA_GROUP * tq, 1), sink_ref[kh * GQA_GROUP], F32)
        for g in range(1, GQA_GROUP):
            sink = jnp.where(grp == g, sink_ref[kh * GQA_GROUP + g], sink)
        sink = sink * log2e
        ss = []
        for r, mask in zip(k_parts, masks):
            s = lax.dot_general(qg, r[:, ksl], (((1,), (1,)), ((), ())), preferred_element_type=F32) * c
            ss.append(s if mask is None else jnp.where(mask, s, NEG_INF))
        m = jnp.maximum(jnp.max(_fold_lanes(ss, jnp.maximum), axis=-1, keepdims=True), sink)
        ps = [jnp.exp2(s - m) for s in ss]
        l = jnp.sum(_fold_lanes(ps, jnp.add), axis=-1, keepdims=True) + jnp.exp2(sink - m)
        o = None
        for p, r in zip(ps, v_parts):
            pv = jnp.dot(p.astype(BF16), r[:, ksl], preferred_element_type=F32)
            o = pv if o is None else o + pv
        o = o / l
        for g in range(GQA_GROUP):
            h = kh * GQA_GROUP + g
            o_ref[:, h * HEAD_DIM:(h + 1) * HEAD_DIM] = o[g * tq:(g + 1) * tq].astype(o_ref.dtype)


def _attn_kernel(sink_ref, q_ref, kp_ref, kc_ref, kn_ref, vp_ref, vc_ref, vn_ref, kx_ref, vx_ref, o_ref):
    n = pl.program_id(1)
    nblk = pl.num_programs(1)
    tq = ATT_BLOCK
    r = lax.broadcasted_iota(jnp.int32, (GQA_GROUP * tq, tq), 0) % tq
    col = lax.broadcasted_iota(jnp.int32, (GQA_GROUP * tq, tq), 1)
    mask_prev = (col >= r) & (n > 0)
    mask_next = (col <= r) & (n < nblk - 1)
    _attn_heads(q_ref, [kp_ref, kc_ref, kn_ref, kx_ref], [vp_ref, vc_ref, vn_ref, vx_ref], sink_ref, o_ref,
                [mask_prev, None, mask_next, None])


def _attn_call(qkv, qkv_ctx, sink):
    b, s, _ = qkv.shape
    n_ctx = qkv_ctx.shape[1]
    tq = ATT_BLOCK
    nblk = s // tq
    qc, kc = N_HEADS * HEAD_DIM, N_KV_HEADS * HEAD_DIM
    kblk, vblk = qc // kc, qc // kc + 1
    kv_spec = lambda off, cb: pl.BlockSpec(
        (None, tq, kc), lambda bb, n: (bb, jnp.clip(n + off, 0, nblk - 1), cb))
    return pl.pallas_call(
        _attn_kernel,
        grid=(b, nblk),
        in_specs=[pl.BlockSpec(memory_space=pltpu.SMEM),
                  pl.BlockSpec((None, tq, qc), lambda bb, n: (bb, n, 0)),
                  kv_spec(-1, kblk), kv_spec(0, kblk), kv_spec(1, kblk),
                  kv_spec(-1, vblk), kv_spec(0, vblk), kv_spec(1, vblk),
                  pl.BlockSpec((None, n_ctx, kc), lambda bb, n: (bb, 0, kblk)),
                  pl.BlockSpec((None, n_ctx, kc), lambda bb, n: (bb, 0, vblk))],
        out_specs=pl.BlockSpec((None, tq, qc), lambda bb, n: (bb, n, 0)),
        out_shape=jax.ShapeDtypeStruct((b, s, qc), BF16),
        compiler_params=_cparams(("parallel", "parallel")),
        name="attn",
    )(sink, qkv, qkv, qkv, qkv, qkv, qkv, qkv, qkv_ctx, qkv_ctx)


def _attn_ctx_kernel(sink_ref, q_ref, kx_ref, vx_ref, o_ref):
    _attn_heads(q_ref, [kx_ref], [vx_ref], sink_ref, o_ref, [None])


def _attn_ctx_call(qkv_ctx, sink):
    b, n_ctx, _ = qkv_ctx.shape
    qc, kc = N_HEADS * HEAD_DIM, N_KV_HEADS * HEAD_DIM
    tq = _tile(n_ctx, ATT_BLOCK)
    return pl.pallas_call(
        _attn_ctx_kernel,
        grid=(b, n_ctx // tq),
        in_specs=[pl.BlockSpec(memory_space=pltpu.SMEM),
                  pl.BlockSpec((None, tq, qc), lambda bb, n: (bb, n, 0)),
                  pl.BlockSpec((None, n_ctx, kc), lambda bb, n: (bb, 0, qc // kc)),
                  pl.BlockSpec((None, n_ctx, kc), lambda bb, n: (bb, 0, qc // kc + 1))],
        out_specs=pl.BlockSpec((None, tq, qc), lambda bb, n: (bb, n, 0)),
        out_shape=jax.ShapeDtypeStruct((b, n_ctx, qc), BF16),
        compiler_params=_cparams(("parallel", "parallel")),
        name="attn_ctx",
    )(sink, qkv_ctx, qkv_ctx, qkv_ctx)


def _oproj_kernel(a_ref, w_ref, x_ref, mod_ref, o_ref, *, d, gate_idx):
    y = jnp.dot(a_ref[...], w_ref[...], preferred_element_type=F32)
    o_ref[...] = x_ref[...] + _mod_slice(mod_ref, gate_idx, d) * y


def _oproj_call(a, w, x, mod, mod_row, *, gate_idx=2, tm_pref=512, name="oproj"):
    nb, s, d = x.shape
    k = a.shape[-1]
    tm = _tile(s, tm_pref)
    mod_map = (lambda b, i: (b, 0, 0)) if mod_row is None else (lambda b, i: (mod_row, 0, 0))
    return pl.pallas_call(
        functools.partial(_oproj_kernel, d=d, gate_idx=gate_idx),
        grid=(nb, s // tm),
        in_specs=[pl.BlockSpec((None, tm, k), lambda b, i: (b, i, 0)),
                  pl.BlockSpec((k, d), lambda b, i: (0, 0)),
                  pl.BlockSpec((None, tm, d), lambda b, i: (b, i, 0)),
                  pl.BlockSpec((None, 1, mod.shape[-1]), mod_map)],
        out_specs=pl.BlockSpec((None, tm, d), lambda b, i: (b, i, 0)),
        out_shape=jax.ShapeDtypeStruct((nb, s, d), F32),
        compiler_params=_cparams(("parallel", "parallel")),
        name=name,
    )(a, w, x, mod)


FFN_CHUNK = 1024
FFN_SUB = 512


def _ffn_kernel(*refs, d, tm, ff, n_i, final_norm):
    it = iter(refs)
    xp_ref, x_ref, xn_ref, mod_ref, g_ref, cw_ref, cb_ref = (next(it) for _ in range(7))
    fg_ref = next(it) if final_norm else None
    wup_hbm, wdn_hbm, o_ref = next(it), next(it), next(it)
    h_ref, acc_ref, wg_buf, wv_buf, wd_buf, sem = (next(it) for _ in range(6))
    t, n_t = pl.program_id(0), pl.num_programs(0)
    i = t % n_i
    halo = SUBLANES
    n_full, tail = ff // FFN_CHUNK, ff % FFN_CHUNK
    n_chunks = n_full + (1 if tail else 0)
    cross_tile = n_chunks % 2 == 0

    def copies(c0, width, slot):
        return (pltpu.make_async_copy(wup_hbm.at[:, pl.ds(c0, width)], wg_buf.at[slot, :, pl.ds(0, width)],
                                      sem.at[slot, 0]),
                pltpu.make_async_copy(wup_hbm.at[:, pl.ds(ff + c0, width)], wv_buf.at[slot, :, pl.ds(0, width)],
                                      sem.at[slot, 1]),
                pltpu.make_async_copy(wdn_hbm.at[pl.ds(c0, width), :], wd_buf.at[slot, pl.ds(0, width), :],
                                      sem.at[slot, 2]))

    def start(c0, width, slot):
        for cp in copies(c0, width, slot):
            cp.start()

    def wait(c0, width, slot):
        for cp in copies(c0, width, slot):
            cp.wait()

    first_w = FFN_CHUNK if n_full else tail
    if cross_tile:
        @pl.when(t == 0)
        def _():
            start(0, first_w, 0)
    else:
        start(0, first_w, 0)

    xs = jnp.concatenate([xp_ref[...], x_ref[...], xn_ref[...]], axis=0)
    h_ref[...] = _rms_mod(xs, g_ref[...], _mod_slice(mod_ref, 3, d), _mod_slice(mod_ref, 4, d)).astype(BF16)
    acc_ref[...] = jnp.zeros_like(acc_ref)
    row = lax.broadcasted_iota(jnp.int32, (tm + 2 * halo, 1), 0)
    keep = ((row >= halo) | (i > 0)) & ((row < tm + halo) | (i < n_i - 1))

    def compute(c0, width, slot):
        h = h_ref[...]
        for j in range(0, width, FFN_SUB):
            gate = jnp.dot(h, wg_buf[slot, :, j:j + FFN_SUB], preferred_element_type=F32)
            val = jnp.dot(h, wv_buf[slot, :, j:j + FFN_SUB], preferred_element_type=F32)[halo:halo + tm]
            gate = jnp.where(keep, gate, 0.0)
            cols = pl.ds(pl.multiple_of(c0 + j, FFN_SUB), FFN_SUB)
            cw = cw_ref[:, cols]
            gc = cb_ref[:, cols]
            for k in range(FFN_CONV):
                off = halo + k - FFN_CONV // 2
                gc = gc + cw[k:k + 1] * gate[off:off + tm]
            act = (_silu(gc) * val).astype(BF16)
            acc_ref[...] += jnp.dot(act, wd_buf[slot, j:j + FFN_SUB, :], preferred_element_type=F32)

    def full_chunk(k, carry):
        slot = k % 2
        c0 = k * FFN_CHUNK

        @pl.when(k + 1 < n_full)
        def _():
            start(c0 + FFN_CHUNK, FFN_CHUNK, 1 - slot)

        if tail:
            @pl.when(k + 1 == n_full)
            def _():
                start(n_full * FFN_CHUNK, tail, 1 - slot)
        elif cross_tile:
            @pl.when((k + 1 == n_full) & (t + 1 < n_t))
            def _():
                start(0, first_w, 0)

        wait(c0, FFN_CHUNK, slot)
        compute(c0, FFN_CHUNK, slot)
        return carry

    lax.fori_loop(0, n_full, full_chunk, 0)
    if tail:
        if cross_tile:
            @pl.when(t + 1 < n_t)
            def _():
                start(0, first_w, 0)
        wait(n_full * FFN_CHUNK, tail, n_full % 2)
        compute(n_full * FFN_CHUNK, tail, n_full % 2)

    out = x_ref[...] + _mod_slice(mod_ref, 5, d) * acc_ref[...]
    if final_norm:
        ms = jnp.mean(out * out, axis=-1, keepdims=True)
        out = out * lax.rsqrt(ms + NORM_EPS) * fg_ref[...]
    o_ref[...] = out


def _ffn_call(x, mod, mod_row, norm_g, w_up, conv_w, conv_b, w_down, final_g=None, *, tm_pref=512, name="ffn"):
    nb, s, d = x.shape
    ff = w_down.shape[0]
    assert ff % FFN_SUB == 0
    tm = _tile(s, tm_pref)
    n_i = s // tm
    r8 = tm // SUBLANES
    bi = lambda t: (t // n_i, t % n_i)
    mod_map = (lambda t: (t // n_i, 0, 0)) if mod_row is None else (lambda t: (mod_row, 0, 0))
    const = lambda shape: pl.BlockSpec(shape, lambda t: (0, 0), pipeline_mode=pl.Buffered(1))
    args = [x, x, x, mod, norm_g.reshape(1, d), conv_w, conv_b.reshape(1, ff)]
    specs = [pl.BlockSpec((None, SUBLANES, d), lambda t: (bi(t)[0], jnp.maximum(bi(t)[1] * r8 - 1, 0), 0)),
             pl.BlockSpec((None, tm, d), lambda t: (bi(t)[0], bi(t)[1], 0)),
             pl.BlockSpec((None, SUBLANES, d),
                          lambda t: (bi(t)[0], jnp.minimum((bi(t)[1] + 1) * r8, s // SUBLANES - 1), 0)),
             pl.BlockSpec((None, 1, mod.shape[-1]), mod_map),
             const((1, d)), const((FFN_CONV, ff)), const((1, ff))]
    if final_g is not None:
        args.append(final_g.reshape(1, d))
        specs.append(const((1, d)))
    args += [w_up, w_down]
    specs += [pl.BlockSpec(memory_space=pl.ANY), pl.BlockSpec(memory_space=pl.ANY)]
    cw = min(FFN_CHUNK, ff)
    return pl.pallas_call(
        functools.partial(_ffn_kernel, d=d, tm=tm, ff=ff, n_i=n_i, final_norm=final_g is not None),
        grid=(nb * n_i,),
        in_specs=specs,
        out_specs=pl.BlockSpec((None, tm, d), lambda t: (bi(t)[0], bi(t)[1], 0)),
        out_shape=jax.ShapeDtypeStruct((nb, s, d), F32),
        scratch_shapes=[pltpu.VMEM((tm + 2 * SUBLANES, d), BF16), pltpu.VMEM((tm, d), F32),
                        pltpu.VMEM((2, d, cw), BF16), pltpu.VMEM((2, d, cw), BF16), pltpu.VMEM((2, cw, d), BF16),
                        pltpu.SemaphoreType.DMA((2, 3))],
        compiler_params=_cparams(("arbitrary",)),
        name=name,
    )(*args)


LRU_SEG = SUBLANES
LRU_CH = 32


def _lru_pitch(n):
    p = n // LRU_SEG + SUBLANES
    return p if (p // SUBLANES) % 2 == 1 else p + SUBLANES


def _fill_pitched(xp_ref, x_ref, n):
    seg, pitch = n // LRU_SEG, _lru_pitch(n)
    lanes = x_ref.shape[1]
    zero = jnp.zeros((SUBLANES, lanes), F32)
    row = lax.broadcasted_iota(jnp.int32, (SUBLANES, lanes), 0)
    xp_ref[0:SUBLANES] = zero
    for j in range(LRU_SEG):
        base = SUBLANES + j * pitch
        xp_ref[base:base + seg] = x_ref[j * seg:(j + 1) * seg]
        tail = x_ref[(j + 1) * seg - SUBLANES:(j + 1) * seg]
        head = x_ref[(j + 1) * seg:(j + 1) * seg + SUBLANES] if j + 1 < LRU_SEG else zero
        if pitch == seg + SUBLANES:
            xp_ref[base + seg:base + pitch] = jnp.where(row < SUBLANES // 2, head, tail)
        else:
            xp_ref[base + seg:base + seg + SUBLANES] = head
            xp_ref[base + pitch - SUBLANES:base + pitch] = tail


def _seg_rows(step, pitch):
    return pl.ds(SUBLANES + step, LRU_SEG, stride=pitch)


def _lru_chunk(xp_ref, hs_ref, ac_ref, pitch, s0, rev, par, h, ac):
    cwb, cbb, wa, ba, wx, bx, kq = par
    lo = 0 if rev else -(LRU_CONV - 1)
    xs = [xp_ref[_seg_rows(s0 + lo + i, pitch), :] for i in range(LRU_CH + LRU_CONV - 1)]
    ucs = []
    for s in range(LRU_CH):
        u = cbb
        for k in range(LRU_CONV):
            u = u + cwb[k] * xs[s + k]
        ucs.append(u)
    uc = jnp.concatenate(ucs, axis=0)
    ucb = uc.astype(BF16)
    ta = jnp.tanh(0.5 * (jnp.dot(ucb, wa, preferred_element_type=F32) + ba))
    tx = jnp.tanh(0.5 * (jnp.dot(ucb, wx, preferred_element_type=F32) + bx))
    t = jnp.tanh(kq + kq * ta)
    q = 1.0 / (1.0 - t)
    a = (1.0 + t) * q
    bt = (2.0 * jnp.sqrt(-t) * q) * ((0.5 + 0.5 * tx) * uc)
    hs, acs = [None] * LRU_CH, [None] * LRU_CH
    for s in (range(LRU_CH - 1, -1, -1) if rev else range(LRU_CH)):
        a_s = a[s * LRU_SEG:(s + 1) * LRU_SEG]
        h = a_s * h + bt[s * LRU_SEG:(s + 1) * LRU_SEG]
        ac = a_s * ac
        hs[s], acs[s] = h, ac
    r0 = pl.multiple_of(s0 * LRU_SEG, LRU_CH * LRU_SEG)
    hs_ref[pl.ds(r0, LRU_CH * LRU_SEG), :] = jnp.concatenate(hs, axis=0)
    ac_ref[pl.ds(r0, LRU_CH * LRU_SEG), :] = jnp.concatenate(acs, axis=0)
    return h, ac


def _lru_pass(xp_ref, scr, n, pars):
    seg, pitch = n // LRU_SEG, _lru_pitch(n)
    nch = seg // LRU_CH
    lanes = xp_ref.shape[1]

    def body(ci, carry):
        hf, af, hr, ar = carry
        hf, af = _lru_chunk(xp_ref, scr[0], scr[1], pitch, ci * LRU_CH, False, pars[0], hf, af)
        hr, ar = _lru_chunk(xp_ref, scr[2], scr[3], pitch, (nch - 1 - ci) * LRU_CH, True, pars[1], hr, ar)
        return hf, af, hr, ar

    zero, one = jnp.zeros((LRU_SEG, lanes), F32), jnp.ones((LRU_SEG, lanes), F32)
    hf, af, hr, ar = lax.fori_loop(0, nch, body, (zero, one, zero, one))
    return (hf, af), (hr, ar)


def _segment_carries(h_end, a_end, h_in, rev):
    row = lax.broadcasted_iota(jnp.int32, h_end.shape, 0)
    sh = LRU_SEG - 1 if rev else 1
    ph, pa = pltpu.roll(h_end, sh, 0), pltpu.roll(a_end, sh, 0)
    c = h_in
    for j in (range(LRU_SEG - 2, -1, -1) if rev else range(1, LRU_SEG)):
        c = jnp.where(row == j, ph + pa * pltpu.roll(c, sh, 0), c)
    last = 0 if rev else LRU_SEG - 1
    end = (h_end + a_end * c)[last:last + 1]
    return c, jnp.broadcast_to(end, h_end.shape)


def _lru_finish(hs_f, ac_f, hs_r, ac_r, c_f, c_r, xp_ref, g_ref, z_ref, n):
    seg, pitch = n // LRU_SEG, _lru_pitch(n)
    nch = seg // LRU_CH
    rows = LRU_CH * LRU_SEG
    cf = jnp.concatenate([c_f] * LRU_CH, axis=0)
    cr = jnp.concatenate([c_r] * LRU_CH, axis=0)

    def comb(ci, carry):
        r0 = pl.multiple_of(ci * rows, rows)
        sl = pl.ds(r0, rows)
        tot = (hs_f[sl, :] + ac_f[sl, :] * cf) + (hs_r[sl, :] + ac_r[sl, :] * cr)
        for s in range(LRU_CH):
            xp_ref[_seg_rows(ci * LRU_CH + s, pitch), :] = tot[s * LRU_SEG:(s + 1) * LRU_SEG]
        return carry

    lax.fori_loop(0, nch, comb, 0)
    rt = min(seg, 128)
    for j in range(LRU_SEG):
        def gate(ri, carry, j=j):
            r = pl.multiple_of(ri * rt, rt)
            tot = xp_ref[pl.ds(SUBLANES + j * pitch + r, rt), :]
            z_ref[pl.ds(j * seg + r, rt), :] = (jax.nn.gelu(g_ref[pl.ds(j * seg + r, rt), :]) * tot).astype(z_ref.dtype)
            return carry

        lax.fori_loop(0, seg // rt, gate, 0)


def _lru_kernel(gl_ref, xl_ref, gc_ref, xc_ref, cw_ref, cb_ref, wa_ref, ba_ref, wx_ref, bx_ref, lam_ref,
                zl_ref, zc_ref, xpl_ref, xpc_ref, *scr):
    s, n_ctx = xl_ref.shape[0], xc_ref.shape[0]
    lanes = xl_ref.shape[1]
    lat, ctx = scr[:4], scr[4:]
    _fill_pitched(xpc_ref, xc_ref, n_ctx)
    _fill_pitched(xpl_ref, xl_ref, s)
    pars = []
    for dr in range(2):
        nl = -lam_ref[dr]
        sp = jnp.maximum(nl, 0.0) + jnp.log1p(jnp.exp(-jnp.abs(nl)))
        cw = cw_ref[dr]
        cwb = [jnp.broadcast_to(cw[k:k + 1], (LRU_SEG, lanes)) for k in range(LRU_CONV)]
        pars.append((cwb, jnp.broadcast_to(cb_ref[dr], (LRU_SEG, lanes)), wa_ref[dr], ba_ref[dr], wx_ref[dr],
                     bx_ref[dr], (-0.25 * LRU_C) * sp))
    zero = jnp.zeros((LRU_SEG, lanes), F32)
    ends_c = _lru_pass(xpc_ref, ctx, n_ctx, pars)
    ends_l = _lru_pass(xpl_ref, lat, s, pars)
    c_ctx, c_lat = [], []
    for dr in range(2):
        c, state = _segment_carries(*ends_c[dr], zero, dr == 1)
        c_ctx.append(c)
        c_lat.append(_segment_carries(*ends_l[dr], state, dr == 1)[0])
    _lru_finish(*ctx, *c_ctx, xpc_ref, gc_ref, zc_ref, n_ctx)
    _lru_finish(*lat, *c_lat, xpl_ref, gl_ref, zl_ref, s)


def _lru_call(u, u_ctx, conv_w, conv_b, wa, ba, wx, bx, lam):
    b, s, r2 = u.shape
    r = r2 // 2
    n_ctx = u_ctx.shape[1]
    lanes = LRU_BLOCK
    nblk = r // lanes
    for n in (s, n_ctx):
        assert n % (LRU_SEG * LRU_CH) == 0, n
    vec = lambda v: v.reshape(2, 1, r)
    col = lambda n, off: pl.BlockSpec((None, n, lanes), lambda bb, c: (bb, 0, c + off))
    par = lambda rows: pl.BlockSpec((2, rows, lanes), lambda bb, c: (0, 0, c))
    wsp = pl.BlockSpec((2, None, lanes, lanes), lambda bb, c: (0, c, 0, 0))
    pitched = lambda n: pltpu.VMEM((SUBLANES + LRU_SEG * _lru_pitch(n), lanes), F32)
    return pl.pallas_call(
        _lru_kernel,
        grid=(b, nblk),
        in_specs=[col(s, 0), col(s, nblk), col(n_ctx, 0), col(n_ctx, nblk),
                  par(LRU_CONV), par(1), wsp, par(1), wsp, par(1), par(1)],
        out_specs=[col(s, 0), col(n_ctx, 0)],
        out_shape=[jax.ShapeDtypeStruct((b, s, r), BF16), jax.ShapeDtypeStruct((b, n_ctx, r), BF16)],
        scratch_shapes=[pitched(s), pitched(n_ctx)] + [pltpu.VMEM((s, lanes), F32)] * 4
                       + [pltpu.VMEM((n_ctx, lanes), F32)] * 4,
        compiler_params=_cparams(("parallel", "parallel")),
        name="lru",
    )(u, u, u_ctx, u_ctx, conv_w, vec(conv_b), wa, vec(ba), wx, vec(bx), vec(lam))


CONF_HALO = 16
CONF_ROWS = 64
CONF_PARTS = 2


def _conf_kernel(zp_ref, z_ref, zn_ref, x_ref, mod_ref, dw_ref, db_ref, lg_ref, lb_ref, wo_ref, bo_ref,
                 o_ref, zw_ref, y_ref, *, d, tm):
    i, n_i = pl.program_id(1), pl.num_programs(1)
    n_slab = d // LANES
    zprev = jnp.where(i > 0, zp_ref[...], 0.0)
    znext = jnp.where(i < n_i - 1, zn_ref[...], 0.0)
    for c in range(n_slab):
        sl = slice(c * LANES, (c + 1) * LANES)
        zw_ref[c, 0:CONF_HALO] = zprev[:, sl]
        zw_ref[c, CONF_HALO:CONF_HALO + tm] = z_ref[:, sl]
        zw_ref[c, CONF_HALO + tm:2 * CONF_HALO + tm] = znext[:, sl]
    base = CONF_HALO - CONF_WIDTH // 2
    rows = min(CONF_ROWS, tm)
    pair = 2 * SUBLANES
    n_ld = rows + pair

    def slab(c, carry):
        l0 = pl.multiple_of(c * LANES, LANES)
        w = dw_ref[:, pl.ds(l0, LANES)]
        wb = [jnp.broadcast_to(w[k:k + 1], (SUBLANES, LANES)) for k in range(CONF_WIDTH)]
        bias = jnp.broadcast_to(db_ref[:, pl.ds(l0, LANES)], (SUBLANES, LANES))

        def block(rb, carry2):
            r0 = rb * rows
            z2 = [zw_ref[c, pl.ds(r0 + base + a, SUBLANES, stride=2), :] for a in range(n_ld)]
            for p in range(rows // pair):
                for odd in range(2):
                    parts = [None] * CONF_PARTS
                    for k in range(CONF_WIDTH):
                        term = wb[k] * z2[p * pair + odd + k]
                        q = k % CONF_PARTS
                        parts[q] = term if parts[q] is None else parts[q] + term
                    acc = bias
                    for q in range(CONF_PARTS):
                        acc = acc + parts[q]
                    y_ref[c, pl.ds(r0 + p * pair + odd, SUBLANES, stride=2), :] = acc
            return carry2

        lax.fori_loop(0, tm // rows, block, 0)
        return carry

    lax.fori_loop(0, n_slab, slab, 0)
    y = jnp.concatenate([y_ref[c] for c in range(n_slab)], axis=1)
    mu = jnp.mean(y, axis=-1, keepdims=True)
    yc = y - mu
    var = jnp.mean(yc * yc, axis=-1, keepdims=True)
    y = _silu(yc * lax.rsqrt(var + NORM_EPS) * lg_ref[...] + lb_ref[...])
    out = jnp.dot(y.astype(BF16), wo_ref[...], preferred_element_type=F32) + bo_ref[...]
    o_ref[...] = x_ref[...] + _mod_slice(mod_ref, 2, d) * out


def _conf_call(z, x, mod, mod_row, dw_w, dw_b, ln_g, ln_b, w_out, b_out, *, tm_pref=512, name="conf"):
    nb, s, d = x.shape
    tm = _tile(s, tm_pref)
    rh = tm // CONF_HALO
    row = lambda v: v.reshape(1, d)
    mod_map = (lambda b, i: (b, 0, 0)) if mod_row is None else (lambda b, i: (mod_row, 0, 0))
    const = lambda shape: pl.BlockSpec(shape, lambda b, i: (0, 0), pipeline_mode=pl.Buffered(1))
    return pl.pallas_call(
        functools.partial(_conf_kernel, d=d, tm=tm),
        grid=(nb, s // tm),
        in_specs=[pl.BlockSpec((None, CONF_HALO, d), lambda b, i: (b, jnp.maximum(i * rh - 1, 0), 0)),
                  pl.BlockSpec((None, tm, d), lambda b, i: (b, i, 0)),
                  pl.BlockSpec((None, CONF_HALO, d),
                               lambda b, i: (b, jnp.minimum((i + 1) * rh, s // CONF_HALO - 1), 0)),
                  pl.BlockSpec((None, tm, d), lambda b, i: (b, i, 0)),
                  pl.BlockSpec((None, 1, mod.shape[-1]), mod_map),
                  const((CONF_WIDTH, d)), const((1, d)), const((1, d)), const((1, d)),
                  const((d, d)), const((1, d))],
        out_specs=pl.BlockSpec((None, tm, d), lambda b, i: (b, i, 0)),
        out_shape=jax.ShapeDtypeStruct((nb, s, d), F32),
        scratch_shapes=[pltpu.VMEM((d // LANES, tm + 2 * CONF_HALO, LANES), F32),
                        pltpu.VMEM((d // LANES, tm, LANES), F32)],
        compiler_params=_cparams(("parallel", "parallel")),
        name=name,
    )(z, z, z, x, mod, dw_w, row(dw_b), row(ln_g), row(ln_b), w_out, row(b_out))


def kernel(x, c, ctx, c_ctx, ada_w, ada_b, norm_mix_g, norm_ffn_g, attn_w_qkv, attn_w_o, attn_sink, lru_w_in, lru_conv_w, lru_conv_b, lru_wa, lru_ba, lru_wx, lru_bx, lru_lambda, lru_w_out, conf_w_in, conf_b_in, conf_dw_w, conf_dw_b, conf_ln_g, conf_ln_b, conf_w_out, conf_b_out, ffn_w_up, ffn_conv_w, ffn_conv_b, ffn_w_down, final_norm_g):
    bsz, seq, d = x.shape
    depth = ada_w.shape[0]
    assert bsz + 1 <= MOD_ROWS
    ctx_row = bsz
    bf = lambda w: w.astype(BF16)

    cvec = jnp.zeros((MOD_ROWS, d), F32).at[:bsz].set(c).at[ctx_row].set(c_ctx)
    mods = _ada_call(cvec, ada_w, ada_b)
    rope = _rope_tables(seq)
    q_cols = N_HEADS * HEAD_DIM
    k_cols = N_KV_HEADS * HEAD_DIM
    xc = ctx

    for i in range(depth):
        last = i == depth - 1
        kind, j = i % N_MIXERS, i // N_MIXERS
        mod = mods[i][:, None, :]
        g_mix = norm_mix_g[i]
        ctx_used = (not last) or kind != 2
        if kind == 0:
            w_qkv, w_o = bf(attn_w_qkv[j]), bf(attn_w_o[j])
            qkv = _proj_call(x, mod, None, g_mix, w_qkv, rope=rope, rope_cols=q_cols + k_cols, out_dtype=BF16,
                             name="qkv")
            qkv_c = _proj_call(xc, mod, ctx_row, g_mix, w_qkv, out_dtype=BF16, name="qkv_ctx")
            o = _attn_call(qkv, qkv_c, attn_sink[j])
            x = _oproj_call(o, w_o, x, mod, None, name="attn_out")
            if not last:
                o_c = _attn_ctx_call(qkv_c, attn_sink[j])
                xc = _oproj_call(o_c, w_o, xc, mod, ctx_row, name="attn_out_ctx")
        elif kind == 1:
            w_in, w_out = bf(lru_w_in[j]), bf(lru_w_out[j])
            u = _proj_call(x, mod, None, g_mix, w_in, name="lru_in")
            u_c = _proj_call(xc, mod, ctx_row, g_mix, w_in, name="lru_in_ctx")
            z, z_c = _lru_call(u, u_c, lru_conv_w[j], lru_conv_b[j], bf(lru_wa[j]), lru_ba[j], bf(lru_wx[j]),
                               lru_bx[j], lru_lambda[j])
            x = _oproj_call(z, w_out, x, mod, None, name="lru_out")
            if not last:
                xc = _oproj_call(z_c, w_out, xc, mod, ctx_row, name="lru_out_ctx")
        else:
            w_in, w_out = bf(conf_w_in[j]), bf(conf_w_out[j])
            tail = (conf_dw_w[j], conf_dw_b[j], conf_ln_g[j], conf_ln_b[j], w_out, conf_b_out[j])
            zz = _proj_call(x, mod, None, g_mix, w_in, conf_b_in[j], glu=True, name="conf_in")
            x = _conf_call(zz, x, mod, None, *tail, name="conf_tail")
            if ctx_used and not last:
                zz_c = _proj_call(xc, mod, ctx_row, g_mix, w_in, conf_b_in[j], glu=True, name="conf_in_ctx")
                xc = _conf_call(zz_c, xc, mod, ctx_row, *tail, name="conf_tail_ctx")
        ffn_w = (norm_ffn_g[i], bf(ffn_w_up[i]), ffn_conv_w[i], ffn_conv_b[i], bf(ffn_w_down[i]))
        x = _ffn_call(x, mod, None, *ffn_w, final_g=final_norm_g if last else None, name="ffn")
        if not last:
            xc = _ffn_call(xc, mod, ctx_row, *ffn_w, name="ffn_ctx")
    return x
```

```python
import functools

import jax
import jax.numpy as jnp
from jax import lax
from jax.experimental import pallas as pl
from jax.experimental.pallas import tpu as pltpu

F32 = jnp.float32
BF16 = jnp.bfloat16

HEAD_DIM = 128
N_HEADS = 16
N_KV_HEADS = 4
GQA_GROUP = N_HEADS // N_KV_HEADS
ATT_BLOCK = 128
GRID_W = 64
ROPE_THETA = 10000.0
ROPE_FREQS = HEAD_DIM // 4
LRU_BLOCK = 128
LRU_C = 8.0
LRU_CONV = 4
CONF_WIDTH = 31
FFN_CONV = 3
NORM_EPS = 1e-6
NEG_INF = -1e30
N_MIXERS = 3

LANES = 128
SUBLANES = 8
ROW_STEP = 16
MOD_ROWS = 8
VMEM_LIMIT = 56 * 1024 * 1024


def _cparams(sem):
    return pltpu.CompilerParams(dimension_semantics=sem, vmem_limit_bytes=VMEM_LIMIT)


def _tile(n, pref):
    t = min(n, pref)
    assert n % t == 0, (n, t)
    return t


def _silu(x):
    return x * jax.nn.sigmoid(x)


def _rms_mod(x, g, shift, scale):
    ms = jnp.mean(x * x, axis=-1, keepdims=True)
    return (x * lax.rsqrt(ms + NORM_EPS) * g) * (1.0 + scale) + shift


def _mod_slice(mod_ref, k, d):
    return mod_ref[:, k * d:(k + 1) * d]


def _ada_kernel(c_ref, w_ref, b_ref, o_ref):
    s = _silu(c_ref[...]).astype(BF16)
    o_ref[...] = jnp.dot(s, w_ref[...].astype(BF16), preferred_element_type=F32) + b_ref[...]


def _ada_call(cvec, ada_w, ada_b):
    depth, d, n = ada_w.shape
    tn = _tile(n, 1024)
    return pl.pallas_call(
        _ada_kernel,
        grid=(depth, n // tn),
        in_specs=[pl.BlockSpec((MOD_ROWS, d), lambda l, j: (0, 0)),
                  pl.BlockSpec((None, d, tn), lambda l, j: (l, 0, j)),
                  pl.BlockSpec((None, 1, tn), lambda l, j: (l, 0, j))],
        out_specs=pl.BlockSpec((None, MOD_ROWS, tn), lambda l, j: (l, 0, j)),
        out_shape=jax.ShapeDtypeStruct((depth, MOD_ROWS, n), F32),
        compiler_params=_cparams(("parallel", "parallel")),
        name="ada",
    )(cvec, ada_w, ada_b.reshape(depth, 1, n))


def _rope(y, cos, sp, sm):
    return y * cos + pltpu.roll(y, 32, 1) * sp + pltpu.roll(y, HEAD_DIM - 32, 1) * sm


def _proj_kernel(*refs, d, tn, n_out, has_bias, glu, rope_cols, n_cast):
    it = iter(refs)
    x_ref, mod_ref, g_ref, w_ref = (next(it) for _ in range(4))
    b_ref = next(it) if has_bias else None
    rope_refs = [next(it) for _ in range(3 if rope_cols else 0)]
    cast_in = [next(it) for _ in range(n_cast)]
    o_ref = next(it)
    for src_ref in cast_in:
        next(it)[...] = src_ref[...].astype(BF16)
    h = _rms_mod(x_ref[...], g_ref[...], _mod_slice(mod_ref, 0, d), _mod_slice(mod_ref, 1, d)).astype(BF16)
    if rope_cols:
        cos, sp, sm = (r[...] for r in rope_refs)
    for c0 in range(0, n_out, tn):
        y = jnp.dot(h, w_ref[:, c0:c0 + tn], preferred_element_type=F32)
        if has_bias:
            y = y + b_ref[:, c0:c0 + tn]
        if glu:
            y2 = jnp.dot(h, w_ref[:, n_out + c0:n_out + c0 + tn], preferred_element_type=F32)
            if has_bias:
                y2 = y2 + b_ref[:, n_out + c0:n_out + c0 + tn]
            y = y * jax.nn.sigmoid(y2)
        if c0 < rope_cols:
            for c in range(0, tn, HEAD_DIM):
                o_ref[:, c0 + c:c0 + c + HEAD_DIM] = _rope(y[:, c:c + HEAD_DIM], cos, sp, sm).astype(o_ref.dtype)
        else:
            o_ref[:, c0:c0 + tn] = y.astype(o_ref.dtype)


CAST_BLOCKS = 32


def _proj_call(x, mod, mod_row, norm_g, w, bias=None, *, glu=False, rope=None, rope_cols=0,
               out_dtype=F32, tm_pref=512, tn_pref=512, cast=(), name="proj"):
    nb, s, d = x.shape
    n_w = w.shape[1]
    n_out = n_w // 2 if glu else n_w
    tm, tn = _tile(s, tm_pref), _tile(n_out, tn_pref)
    mod_map = (lambda b, i: (b, 0, 0)) if mod_row is None else (lambda b, i: (mod_row, 0, 0))
    const = lambda shape: pl.BlockSpec(shape, lambda b, i: (0, 0), pipeline_mode=pl.Buffered(1))
    args = [x, mod, norm_g.reshape(1, d), w]
    specs = [pl.BlockSpec((None, tm, d), lambda b, i: (b, i, 0)),
             pl.BlockSpec((None, 1, mod.shape[-1]), mod_map),
             const((1, d)), const((d, n_w))]
    if bias is not None:
        args.append(bias.reshape(1, n_w))
        specs.append(const((1, n_w)))
    if rope is not None:
        assert rope_cols % tn == 0
        for t in rope:
            args.append(t)
            specs.append(pl.BlockSpec((tm, HEAD_DIM), lambda b, i: (i, 0)))
    n_i = s // tm
    n_steps = nb * n_i
    row_tile = 2 * SUBLANES
    ride = bool(cast) and n_steps % CAST_BLOCKS == 0 and all(a.shape[0] % (CAST_BLOCKS * row_tile) == 0 for a in cast)
    out_specs = [pl.BlockSpec((None, tm, n_out), lambda b, i: (b, i, 0))]
    out_shape = [jax.ShapeDtypeStruct((nb, s, n_out), out_dtype)]
    if ride:
        cast_map = lambda b, i: (((b * n_i + i) * CAST_BLOCKS) // n_steps, 0)
        for a in cast:
            blk = (a.shape[0] // CAST_BLOCKS, a.shape[1])
            args.append(a)
            specs.append(pl.BlockSpec(blk, cast_map))
            out_specs.append(pl.BlockSpec(blk, cast_map))
            out_shape.append(jax.ShapeDtypeStruct(a.shape, BF16))
    kern = functools.partial(_proj_kernel, d=d, tn=tn, n_out=n_out, has_bias=bias is not None, glu=glu,
                             rope_cols=rope_cols if rope is not None else 0, n_cast=len(cast) if ride else 0)
    outs = pl.pallas_call(
        kern,
        grid=(nb, n_i),
        in_specs=specs,
        out_specs=out_specs,
        out_shape=out_shape,
        compiler_params=_cparams(("arbitrary", "arbitrary") if ride else ("parallel", "parallel")),
        name=name,
    )(*args)
    return outs[0], (list(outs[1:]) if ride else [a.astype(BF16) for a in cast])


def _rope_tables(s):
    t = jnp.arange(s)
    row = (t // GRID_W).astype(F32)
    col = (t % GRID_W).astype(F32)
    freq = ROPE_THETA ** (-jnp.arange(ROPE_FREQS, dtype=F32) / ROPE_FREQS)
    dd = jnp.arange(HEAD_DIM)
    axis, half, f = dd // (2 * ROPE_FREQS), (dd % (2 * ROPE_FREQS)) // ROPE_FREQS, dd % ROPE_FREQS
    pos = jnp.where(axis[None, :] == 0, row[:, None], col[:, None])
    ang = pos * freq[f][None, :]
    cos, sin = jnp.cos(ang), jnp.sin(ang)
    sp = jnp.where(half[None, :] == 1, sin, 0.0)
    sm = jnp.where(half[None, :] == 0, -sin, 0.0)
    return cos, sp, sm


def _fold_lanes(parts, op):
    tiles = [p[:, j:j + LANES] for p in parts for j in range(0, p.shape[1], LANES)]
    acc = tiles[0]
    for t in tiles[1:]:
        acc = op(acc, t)
    return acc


ATT_HEADS_PER_UNIT = 4


def _attn_heads(q_ref, k_parts, v_parts, sink_ref, o_ref, masks):
    tq = q_ref.shape[0]
    hpu = ATT_HEADS_PER_UNIT
    log2e = 1.4426950408889634
    c = (HEAD_DIM ** -0.5) * log2e
    grp = lax.broadcasted_iota(jnp.int32, (hpu * tq, 1), 0) // tq
    units = [(h0 // GQA_GROUP, h0) for h0 in range(0, N_HEADS, hpu)]

    def scores(kh, h0):
        ksl = slice(kh * HEAD_DIM, (kh + 1) * HEAD_DIM)
        qg = jnp.concatenate([q_ref[:, (h0 + g) * HEAD_DIM:(h0 + g + 1) * HEAD_DIM] for g in range(hpu)], axis=0)
        out = []
        for r, mask in zip(k_parts, masks):
            s = lax.dot_general(qg, r[:, ksl], (((1,), (1,)), ((), ())), preferred_element_type=F32) * c
            out.append(s if mask is None else jnp.where(mask, s, NEG_INF))
        return out

    def finish(kh, h0, ps, l):
        ksl = slice(kh * HEAD_DIM, (kh + 1) * HEAD_DIM)
        o = None
        for p, r in zip(ps, v_parts):
            pv = jnp.dot(p.astype(BF16), r[:, ksl], preferred_element_type=F32)
            o = pv if o is None else o + pv
        o = o / l
        for g in range(hpu):
            o_ref[:, (h0 + g) * HEAD_DIM:(h0 + g + 1) * HEAD_DIM] = o[g * tq:(g + 1) * tq].astype(o_ref.dtype)

    nxt = scores(*units[0])
    pending = None
    for u, (kh, h0) in enumerate(units):
        ss = nxt
        if u + 1 < len(units):
            nxt = scores(*units[u + 1])
        if pending is not None:
            finish(*pending)
        sink = jnp.full((hpu * tq, 1), sink_ref[h0], F32)
        for g in range(1, hpu):
            sink = jnp.where(grp == g, sink_ref[h0 + g], sink)
        sink = sink * log2e
        m = jnp.maximum(jnp.max(_fold_lanes(ss, jnp.maximum), axis=-1, keepdims=True), sink)
        ps = [jnp.exp2(s - m) for s in ss]
        l = jnp.sum(_fold_lanes(ps, jnp.add), axis=-1, keepdims=True) + jnp.exp2(sink - m)
        pending = (kh, h0, ps, l)
    finish(*pending)


def _attn_kernel(sink_ref, q_ref, kp_ref, kc_ref, kn_ref, vp_ref, vc_ref, vn_ref, kx_ref, vx_ref, o_ref):
    n = pl.program_id(1)
    nblk = pl.num_programs(1)
    tq = ATT_BLOCK
    r = lax.broadcasted_iota(jnp.int32, (ATT_HEADS_PER_UNIT * tq, tq), 0) % tq
    col = lax.broadcasted_iota(jnp.int32, (ATT_HEADS_PER_UNIT * tq, tq), 1)
    mask_prev = (col >= r) & (n > 0)
    mask_next = (col <= r) & (n < nblk - 1)
    _attn_heads(q_ref, [kp_ref, kc_ref, kn_ref, kx_ref], [vp_ref, vc_ref, vn_ref, vx_ref], sink_ref, o_ref,
                [mask_prev, None, mask_next, None])


def _attn_call(qkv, qkv_ctx, sink):
    b, s, _ = qkv.shape
    n_ctx = qkv_ctx.shape[1]
    tq = ATT_BLOCK
    nblk = s // tq
    qc, kc = N_HEADS * HEAD_DIM, N_KV_HEADS * HEAD_DIM
    kblk, vblk = qc // kc, qc // kc + 1
    kv_spec = lambda off, cb: pl.BlockSpec(
        (None, tq, kc), lambda bb, n: (bb, jnp.clip(n + off, 0, nblk - 1), cb))
    return pl.pallas_call(
        _attn_kernel,
        grid=(b, nblk),
        in_specs=[pl.BlockSpec(memory_space=pltpu.SMEM),
                  pl.BlockSpec((None, tq, qc), lambda bb, n: (bb, n, 0)),
                  kv_spec(-1, kblk), kv_spec(0, kblk), kv_spec(1, kblk),
                  kv_spec(-1, vblk), kv_spec(0, vblk), kv_spec(1, vblk),
                  pl.BlockSpec((None, n_ctx, kc), lambda bb, n: (bb, 0, kblk)),
                  pl.BlockSpec((None, n_ctx, kc), lambda bb, n: (bb, 0, vblk))],
        out_specs=pl.BlockSpec((None, tq, qc), lambda bb, n: (bb, n, 0)),
        out_shape=jax.ShapeDtypeStruct((b, s, qc), BF16),
        compiler_params=_cparams(("parallel", "parallel")),
        name="attn",
    )(sink, qkv, qkv, qkv, qkv, qkv, qkv, qkv, qkv_ctx, qkv_ctx)


def _attn_ctx_kernel(sink_ref, q_ref, kx_ref, vx_ref, o_ref):
    _attn_heads(q_ref, [kx_ref], [vx_ref], sink_ref, o_ref, [None])


def _attn_ctx_call(qkv_ctx, sink):
    b, n_ctx, _ = qkv_ctx.shape
    qc, kc = N_HEADS * HEAD_DIM, N_KV_HEADS * HEAD_DIM
    tq = _tile(n_ctx, ATT_BLOCK)
    return pl.pallas_call(
        _attn_ctx_kernel,
        grid=(b, n_ctx // tq),
        in_specs=[pl.BlockSpec(memory_space=pltpu.SMEM),
                  pl.BlockSpec((None, tq, qc), lambda bb, n: (bb, n, 0)),
                  pl.BlockSpec((None, n_ctx, kc), lambda bb, n: (bb, 0, qc // kc)),
                  pl.BlockSpec((None, n_ctx, kc), lambda bb, n: (bb, 0, qc // kc + 1))],
        out_specs=pl.BlockSpec((None, tq, qc), lambda bb, n: (bb, n, 0)),
        out_shape=jax.ShapeDtypeStruct((b, n_ctx, qc), BF16),
        compiler_params=_cparams(("parallel", "parallel")),
        name="attn_ctx",
    )(sink, qkv_ctx, qkv_ctx, qkv_ctx)


def _oproj_kernel(a_ref, w_ref, x_ref, mod_ref, o_ref, *, d, gate_idx):
    y = jnp.dot(a_ref[...], w_ref[...], preferred_element_type=F32)
    o_ref[...] = x_ref[...] + _mod_slice(mod_ref, gate_idx, d) * y


def _oproj_call(a, w, x, mod, mod_row, *, gate_idx=2, tm_pref=512, name="oproj"):
    nb, s, d = x.shape
    k = a.shape[-1]
    tm = _tile(s, tm_pref)
    mod_map = (lambda b, i: (b, 0, 0)) if mod_row is None else (lambda b, i: (mod_row, 0, 0))
    return pl.pallas_call(
        functools.partial(_oproj_kernel, d=d, gate_idx=gate_idx),
        grid=(nb, s // tm),
        in_specs=[pl.BlockSpec((None, tm, k), lambda b, i: (b, i, 0)),
                  pl.BlockSpec((k, d), lambda b, i: (0, 0)),
                  pl.BlockSpec((None, tm, d), lambda b, i: (b, i, 0)),
                  pl.BlockSpec((None, 1, mod.shape[-1]), mod_map)],
        out_specs=pl.BlockSpec((None, tm, d), lambda b, i: (b, i, 0)),
        out_shape=jax.ShapeDtypeStruct((nb, s, d), F32),
        compiler_params=_cparams(("parallel", "parallel")),
        name=name,
    )(a, w, x, mod)


FFN_CHUNK = 1024
FFN_SUB = 512


def _ffn_kernel(*refs, d, tm, ff, n_i, final_norm):
    it = iter(refs)
    xp_ref, x_ref, xn_ref, mod_ref, g_ref, cw_ref, cb_ref = (next(it) for _ in range(7))
    fg_ref = next(it) if final_norm else None
    wup_hbm, wdn_hbm, o_ref = next(it), next(it), next(it)
    h_ref, acc_ref, wg_buf, wv_buf, wd_buf, sem = (next(it) for _ in range(6))
    t, n_t = pl.program_id(0), pl.num_programs(0)
    i = t % n_i
    halo = SUBLANES
    n_full, tail = ff // FFN_CHUNK, ff % FFN_CHUNK
    n_chunks = n_full + (1 if tail else 0)
    cross_tile = n_chunks % 2 == 0

    def copies(c0, width, slot):
        return (pltpu.make_async_copy(wup_hbm.at[:, pl.ds(c0, width)], wg_buf.at[slot, :, pl.ds(0, width)],
                                      sem.at[slot, 0]),
                pltpu.make_async_copy(wup_hbm.at[:, pl.ds(ff + c0, width)], wv_buf.at[slot, :, pl.ds(0, width)],
                                      sem.at[slot, 1]),
                pltpu.make_async_copy(wdn_hbm.at[pl.ds(c0, width), :], wd_buf.at[slot, pl.ds(0, width), :],
                                      sem.at[slot, 2]))

    def start(c0, width, slot):
        for cp in copies(c0, width, slot):
            cp.start()

    def wait(c0, width, slot):
        for cp in copies(c0, width, slot):
            cp.wait()

    first_w = FFN_CHUNK if n_full else tail
    if cross_tile:
        @pl.when(t == 0)
        def _():
            start(0, first_w, 0)
    else:
        start(0, first_w, 0)

    rows = lambda v: jnp.broadcast_to(v, (ROW_STEP, d))
    g, shift, scale = rows(g_ref[...]), rows(_mod_slice(mod_ref, 3, d)), rows(_mod_slice(mod_ref, 4, d))
    for e0 in range(0, tm + 2 * halo, ROW_STEP):
        if e0 == 0:
            xs = jnp.concatenate([xp_ref[...], x_ref[0:ROW_STEP - halo]], axis=0)
        elif e0 + ROW_STEP > tm + halo:
            xs = jnp.concatenate([x_ref[e0 - halo:tm], xn_ref[...]], axis=0)
        else:
            xs = x_ref[e0 - halo:e0 - halo + ROW_STEP]
        h_ref[e0:e0 + ROW_STEP] = _rms_mod(xs, g, shift, scale).astype(BF16)
    acc_ref[...] = jnp.zeros_like(acc_ref)
    row = lax.broadcasted_iota(jnp.int32, (tm + 2 * halo, 1), 0)
    keep = ((row >= halo) | (i > 0)) & ((row < tm + halo) | (i < n_i - 1))

    def compute(c0, width, slot):
        h = h_ref[...]
        for j in range(0, width, FFN_SUB):
            gate = jnp.dot(h, wg_buf[slot, :, j:j + FFN_SUB], preferred_element_type=F32)
            val = jnp.dot(h, wv_buf[slot, :, j:j + FFN_SUB], preferred_element_type=F32)[halo:halo + tm]
            gate = jnp.where(keep, gate, 0.0)
            cols = pl.ds(pl.multiple_of(c0 + j, FFN_SUB), FFN_SUB)
            cw = cw_ref[:, cols]
            gc = cb_ref[:, cols]
            for k in range(FFN_CONV):
                off = halo + k - FFN_CONV // 2
                gc = gc + cw[k:k + 1] * gate[off:off + tm]
            act = (_silu(gc) * val).astype(BF16)
            acc_ref[...] += jnp.dot(act, wd_buf[slot, j:j + FFN_SUB, :], preferred_element_type=F32)

    def full_chunk(k, carry):
        slot = k % 2
        c0 = k * FFN_CHUNK

        @pl.when(k + 1 < n_full)
        def _():
            start(c0 + FFN_CHUNK, FFN_CHUNK, 1 - slot)

        if tail:
            @pl.when(k + 1 == n_full)
            def _():
                start(n_full * FFN_CHUNK, tail, 1 - slot)
        elif cross_tile:
            @pl.when((k + 1 == n_full) & (t + 1 < n_t))
            def _():
                start(0, first_w, 0)

        wait(c0, FFN_CHUNK, slot)
        compute(c0, FFN_CHUNK, slot)
        return carry

    lax.fori_loop(0, n_full, full_chunk, 0)
    if tail:
        if cross_tile:
            @pl.when(t + 1 < n_t)
            def _():
                start(0, first_w, 0)
        wait(n_full * FFN_CHUNK, tail, n_full % 2)
        compute(n_full * FFN_CHUNK, tail, n_full % 2)

    gate2 = rows(_mod_slice(mod_ref, 5, d))
    fg = rows(fg_ref[...]) if final_norm else None
    for r0 in range(0, tm, ROW_STEP):
        out = x_ref[r0:r0 + ROW_STEP] + gate2 * acc_ref[r0:r0 + ROW_STEP]
        if final_norm:
            ms = jnp.mean(out * out, axis=-1, keepdims=True)
            out = out * lax.rsqrt(ms + NORM_EPS) * fg
        o_ref[r0:r0 + ROW_STEP] = out


def _ffn_call(x, mod, mod_row, norm_g, w_up, conv_w, conv_b, w_down, final_g=None, *, tm_pref=512, name="ffn"):
    nb, s, d = x.shape
    ff = w_down.shape[0]
    assert ff % FFN_SUB == 0
    tm = _tile(s, tm_pref)
    assert tm % ROW_STEP == 0 and ROW_STEP == 2 * SUBLANES
    n_i = s // tm
    r8 = tm // SUBLANES
    bi = lambda t: (t // n_i, t % n_i)
    mod_map = (lambda t: (t // n_i, 0, 0)) if mod_row is None else (lambda t: (mod_row, 0, 0))
    const = lambda shape: pl.BlockSpec(shape, lambda t: (0, 0), pipeline_mode=pl.Buffered(1))
    args = [x, x, x, mod, norm_g.reshape(1, d), conv_w, conv_b.reshape(1, ff)]
    specs = [pl.BlockSpec((None, SUBLANES, d), lambda t: (bi(t)[0], jnp.maximum(bi(t)[1] * r8 - 1, 0), 0)),
             pl.BlockSpec((None, tm, d), lambda t: (bi(t)[0], bi(t)[1], 0)),
             pl.BlockSpec((None, SUBLANES, d),
                          lambda t: (bi(t)[0], jnp.minimum((bi(t)[1] + 1) * r8, s // SUBLANES - 1), 0)),
             pl.BlockSpec((None, 1, mod.shape[-1]), mod_map),
             const((1, d)), const((FFN_CONV, ff)), const((1, ff))]
    if final_g is not None:
        args.append(final_g.reshape(1, d))
        specs.append(const((1, d)))
    args += [w_up, w_down]
    specs += [pl.BlockSpec(memory_space=pl.ANY), pl.BlockSpec(memory_space=pl.ANY)]
    cw = min(FFN_CHUNK, ff)
    return pl.pallas_call(
        functools.partial(_ffn_kernel, d=d, tm=tm, ff=ff, n_i=n_i, final_norm=final_g is not None),
        grid=(nb * n_i,),
        in_specs=specs,
        out_specs=pl.BlockSpec((None, tm, d), lambda t: (bi(t)[0], bi(t)[1], 0)),
        out_shape=jax.ShapeDtypeStruct((nb, s, d), F32),
        scratch_shapes=[pltpu.VMEM((tm + 2 * SUBLANES, d), BF16), pltpu.VMEM((tm, d), F32),
                        pltpu.VMEM((2, d, cw), BF16), pltpu.VMEM((2, d, cw), BF16), pltpu.VMEM((2, cw, d), BF16),
                        pltpu.SemaphoreType.DMA((2, 3))],
        compiler_params=_cparams(("arbitrary",)),
        name=name,
    )(*args)


LRU_SEG = SUBLANES
LRU_CH = 32


def _lru_pitch(n):
    p = n // LRU_SEG + SUBLANES
    return p if (p // SUBLANES) % 2 == 1 else p + SUBLANES


def _fill_pitched(xp_ref, x_ref, n):
    seg, pitch = n // LRU_SEG, _lru_pitch(n)
    lanes = x_ref.shape[1]
    zero = jnp.zeros((SUBLANES, lanes), F32)
    row = lax.broadcasted_iota(jnp.int32, (SUBLANES, lanes), 0)
    xp_ref[0:SUBLANES] = zero
    for j in range(LRU_SEG):
        base = SUBLANES + j * pitch
        xp_ref[base:base + seg] = x_ref[j * seg:(j + 1) * seg]
        tail = x_ref[(j + 1) * seg - SUBLANES:(j + 1) * seg]
        head = x_ref[(j + 1) * seg:(j + 1) * seg + SUBLANES] if j + 1 < LRU_SEG else zero
        if pitch == seg + SUBLANES:
            xp_ref[base + seg:base + pitch] = jnp.where(row < SUBLANES // 2, head, tail)
        else:
            xp_ref[base + seg:base + seg + SUBLANES] = head
            xp_ref[base + pitch - SUBLANES:base + pitch] = tail


def _seg_rows(step, pitch):
    return pl.ds(SUBLANES + step, LRU_SEG, stride=pitch)


def _lru_chunk(xp_ref, hs_ref, ac_ref, pitch, s0, rev, par, h, ac):
    cwb, cbb, wa, ba, wx, bx, kq = par
    lo = 0 if rev else -(LRU_CONV - 1)
    xs = [xp_ref[_seg_rows(s0 + lo + i, pitch), :] for i in range(LRU_CH + LRU_CONV - 1)]
    ucs = []
    for s in range(LRU_CH):
        u = cbb
        for k in range(LRU_CONV):
            u = u + cwb[k] * xs[s + k]
        ucs.append(u)
    uc = jnp.concatenate(ucs, axis=0)
    ucb = uc.astype(BF16)
    ta = jnp.tanh(0.5 * (jnp.dot(ucb, wa, preferred_element_type=F32) + ba))
    tx = jnp.tanh(0.5 * (jnp.dot(ucb, wx, preferred_element_type=F32) + bx))
    t = jnp.tanh(kq + kq * ta)
    q = 1.0 / (1.0 - t)
    a = (1.0 + t) * q
    bt = (2.0 * jnp.sqrt(-t) * q) * ((0.5 + 0.5 * tx) * uc)
    hs, acs = [None] * LRU_CH, [None] * LRU_CH
    for s in (range(LRU_CH - 1, -1, -1) if rev else range(LRU_CH)):
        a_s = a[s * LRU_SEG:(s + 1) * LRU_SEG]
        h = a_s * h + bt[s * LRU_SEG:(s + 1) * LRU_SEG]
        ac = a_s * ac
        hs[s], acs[s] = h, ac
    r0 = pl.multiple_of(s0 * LRU_SEG, LRU_CH * LRU_SEG)
    hs_ref[pl.ds(r0, LRU_CH * LRU_SEG), :] = jnp.concatenate(hs, axis=0)
    ac_ref[pl.ds(r0, LRU_CH * LRU_SEG), :] = jnp.concatenate(acs, axis=0)
    return h, ac


def _lru_pass(xp_ref, scr, n, pars):
    seg, pitch = n // LRU_SEG, _lru_pitch(n)
    nch = seg // LRU_CH
    lanes = xp_ref.shape[1]

    def body(ci, carry):
        hf, af, hr, ar = carry
        hf, af = _lru_chunk(xp_ref, scr[0], scr[1], pitch, ci * LRU_CH, False, pars[0], hf, af)
        hr, ar = _lru_chunk(xp_ref, scr[2], scr[3], pitch, (nch - 1 - ci) * LRU_CH, True, pars[1], hr, ar)
        return hf, af, hr, ar

    zero, one = jnp.zeros((LRU_SEG, lanes), F32), jnp.ones((LRU_SEG, lanes), F32)
    hf, af, hr, ar = lax.fori_loop(0, nch, body, (zero, one, zero, one))
    return (hf, af), (hr, ar)


def _segment_carries(h_end, a_end, h_in, rev):
    row = lax.broadcasted_iota(jnp.int32, h_end.shape, 0)
    sh = LRU_SEG - 1 if rev else 1
    ph, pa = pltpu.roll(h_end, sh, 0), pltpu.roll(a_end, sh, 0)
    c = h_in
    for j in (range(LRU_SEG - 2, -1, -1) if rev else range(1, LRU_SEG)):
        c = jnp.where(row == j, ph + pa * pltpu.roll(c, sh, 0), c)
    last = 0 if rev else LRU_SEG - 1
    end = (h_end + a_end * c)[last:last + 1]
    return c, jnp.broadcast_to(end, h_end.shape)


def _lru_finish(hs_f, ac_f, hs_r, ac_r, c_f, c_r, xp_ref, g_ref, z_ref, n):
    seg, pitch = n // LRU_SEG, _lru_pitch(n)
    nch = seg // LRU_CH
    rows = LRU_CH * LRU_SEG
    cf = jnp.concatenate([c_f] * LRU_CH, axis=0)
    cr = jnp.concatenate([c_r] * LRU_CH, axis=0)

    def comb(ci, carry):
        r0 = pl.multiple_of(ci * rows, rows)
        sl = pl.ds(r0, rows)
        tot = (hs_f[sl, :] + ac_f[sl, :] * cf) + (hs_r[sl, :] + ac_r[sl, :] * cr)
        for s in range(LRU_CH):
            xp_ref[_seg_rows(ci * LRU_CH + s, pitch), :] = tot[s * LRU_SEG:(s + 1) * LRU_SEG]
        return carry

    lax.fori_loop(0, nch, comb, 0)
    rt = min(seg, 128)
    for j in range(LRU_SEG):
        def gate(ri, carry, j=j):
            r = pl.multiple_of(ri * rt, rt)
            tot = xp_ref[pl.ds(SUBLANES + j * pitch + r, rt), :]
            z_ref[pl.ds(j * seg + r, rt), :] = (jax.nn.gelu(g_ref[pl.ds(j * seg + r, rt), :]) * tot).astype(z_ref.dtype)
            return carry

        lax.fori_loop(0, seg // rt, gate, 0)


def _lru_kernel(gl_ref, xl_ref, gc_ref, xc_ref, cw_ref, cb_ref, wa_ref, ba_ref, wx_ref, bx_ref, lam_ref,
                zl_ref, zc_ref, xpl_ref, xpc_ref, *scr):
    s, n_ctx = xl_ref.shape[0], xc_ref.shape[0]
    lanes = xl_ref.shape[1]
    lat, ctx = scr[:4], scr[4:]
    _fill_pitched(xpc_ref, xc_ref, n_ctx)
    _fill_pitched(xpl_ref, xl_ref, s)
    pars = []
    for dr in range(2):
        nl = -lam_ref[dr]
        sp = jnp.maximum(nl, 0.0) + jnp.log1p(jnp.exp(-jnp.abs(nl)))
        cw = cw_ref[dr]
        cwb = [jnp.broadcast_to(cw[k:k + 1], (LRU_SEG, lanes)) for k in range(LRU_CONV)]
        pars.append((cwb, jnp.broadcast_to(cb_ref[dr], (LRU_SEG, lanes)), wa_ref[dr], ba_ref[dr], wx_ref[dr],
                     bx_ref[dr], (-0.25 * LRU_C) * sp))
    zero = jnp.zeros((LRU_SEG, lanes), F32)
    ends_c = _lru_pass(xpc_ref, ctx, n_ctx, pars)
    ends_l = _lru_pass(xpl_ref, lat, s, pars)
    c_ctx, c_lat = [], []
    for dr in range(2):
        c, state = _segment_carries(*ends_c[dr], zero, dr == 1)
        c_ctx.append(c)
        c_lat.append(_segment_carries(*ends_l[dr], state, dr == 1)[0])
    _lru_finish(*ctx, *c_ctx, xpc_ref, gc_ref, zc_ref, n_ctx)
    _lru_finish(*lat, *c_lat, xpl_ref, gl_ref, zl_ref, s)


def _lru_call(u, u_ctx, conv_w, conv_b, wa, ba, wx, bx, lam):
    b, s, r2 = u.shape
    r = r2 // 2
    n_ctx = u_ctx.shape[1]
    lanes = LRU_BLOCK
    nblk = r // lanes
    for n in (s, n_ctx):
        assert n % (LRU_SEG * LRU_CH) == 0, n
    vec = lambda v: v.reshape(2, 1, r)
    col = lambda n, off: pl.BlockSpec((None, n, lanes), lambda bb, c: (bb, 0, c + off))
    par = lambda rows: pl.BlockSpec((2, rows, lanes), lambda bb, c: (0, 0, c))
    wsp = pl.BlockSpec((2, None, lanes, lanes), lambda bb, c: (0, c, 0, 0))
    pitched = lambda n: pltpu.VMEM((SUBLANES + LRU_SEG * _lru_pitch(n), lanes), F32)
    return pl.pallas_call(
        _lru_kernel,
        grid=(b, nblk),
        in_specs=[col(s, 0), col(s, nblk), col(n_ctx, 0), col(n_ctx, nblk),
                  par(LRU_CONV), par(1), wsp, par(1), wsp, par(1), par(1)],
        out_specs=[col(s, 0), col(n_ctx, 0)],
        out_shape=[jax.ShapeDtypeStruct((b, s, r), BF16), jax.ShapeDtypeStruct((b, n_ctx, r), BF16)],
        scratch_shapes=[pitched(s), pitched(n_ctx)] + [pltpu.VMEM((s, lanes), F32)] * 4
                       + [pltpu.VMEM((n_ctx, lanes), F32)] * 4,
        compiler_params=_cparams(("parallel", "parallel")),
        name="lru",
    )(u, u, u_ctx, u_ctx, conv_w, vec(conv_b), wa, vec(ba), wx, vec(bx), vec(lam))


CONF_HALO = 16
CONF_ROWS = 64
CONF_PARTS = 2


def _conf_kernel(zp_ref, z_ref, zn_ref, x_ref, mod_ref, dw_ref, db_ref, lg_ref, lb_ref, wo_ref, bo_ref,
                 o_ref, zw_ref, y_ref, *, d, tm):
    i, n_i = pl.program_id(1), pl.num_programs(1)
    n_slab = d // LANES
    zprev = jnp.where(i > 0, zp_ref[...], 0.0)
    znext = jnp.where(i < n_i - 1, zn_ref[...], 0.0)
    for c in range(n_slab):
        sl = slice(c * LANES, (c + 1) * LANES)
        zw_ref[c, 0:CONF_HALO] = zprev[:, sl]
        zw_ref[c, CONF_HALO:CONF_HALO + tm] = z_ref[:, sl]
        zw_ref[c, CONF_HALO + tm:2 * CONF_HALO + tm] = znext[:, sl]
    base = CONF_HALO - CONF_WIDTH // 2
    rows = min(CONF_ROWS, tm)
    pair = 2 * SUBLANES
    n_ld = rows + pair

    def slab(c, carry):
        l0 = pl.multiple_of(c * LANES, LANES)
        w = dw_ref[:, pl.ds(l0, LANES)]
        wb = [jnp.broadcast_to(w[k:k + 1], (SUBLANES, LANES)) for k in range(CONF_WIDTH)]
        bias = jnp.broadcast_to(db_ref[:, pl.ds(l0, LANES)], (SUBLANES, LANES))

        def block(rb, carry2):
            r0 = rb * rows
            z2 = [zw_ref[c, pl.ds(r0 + base + a, SUBLANES, stride=2), :] for a in range(n_ld)]
            for p in range(rows // pair):
                for odd in range(2):
                    parts = [None] * CONF_PARTS
                    for k in range(CONF_WIDTH):
                        term = wb[k] * z2[p * pair + odd + k]
                        q = k % CONF_PARTS
                        parts[q] = term if parts[q] is None else parts[q] + term
                    acc = bias
                    for q in range(CONF_PARTS):
                        acc = acc + parts[q]
                    y_ref[c, pl.ds(r0 + p * pair + odd, SUBLANES, stride=2), :] = acc
            return carry2

        lax.fori_loop(0, tm // rows, block, 0)
        return carry

    lax.fori_loop(0, n_slab, slab, 0)
    y = jnp.concatenate([y_ref[c] for c in range(n_slab)], axis=1)
    mu = jnp.mean(y, axis=-1, keepdims=True)
    yc = y - mu
    var = jnp.mean(yc * yc, axis=-1, keepdims=True)
    y = _silu(yc * lax.rsqrt(var + NORM_EPS) * lg_ref[...] + lb_ref[...])
    out = jnp.dot(y.astype(BF16), wo_ref[...], preferred_element_type=F32) + bo_ref[...]
    o_ref[...] = x_ref[...] + _mod_slice(mod_ref, 2, d) * out


def _conf_call(z, x, mod, mod_row, dw_w, dw_b, ln_g, ln_b, w_out, b_out, *, tm_pref=512, name="conf"):
    nb, s, d = x.shape
    tm = _tile(s, tm_pref)
    rh = tm // CONF_HALO
    row = lambda v: v.reshape(1, d)
    mod_map = (lambda b, i: (b, 0, 0)) if mod_row is None else (lambda b, i: (mod_row, 0, 0))
    const = lambda shape: pl.BlockSpec(shape, lambda b, i: (0, 0), pipeline_mode=pl.Buffered(1))
    return pl.pallas_call(
        functools.partial(_conf_kernel, d=d, tm=tm),
        grid=(nb, s // tm),
        in_specs=[pl.BlockSpec((None, CONF_HALO, d), lambda b, i: (b, jnp.maximum(i * rh - 1, 0), 0)),
                  pl.BlockSpec((None, tm, d), lambda b, i: (b, i, 0)),
                  pl.BlockSpec((None, CONF_HALO, d),
                               lambda b, i: (b, jnp.minimum((i + 1) * rh, s // CONF_HALO - 1), 0)),
                  pl.BlockSpec((None, tm, d), lambda b, i: (b, i, 0)),
                  pl.BlockSpec((None, 1, mod.shape[-1]), mod_map),
                  const((CONF_WIDTH, d)), const((1, d)), const((1, d)), const((1, d)),
                  const((d, d)), const((1, d))],
        out_specs=pl.BlockSpec((None, tm, d), lambda b, i: (b, i, 0)),
        out_shape=jax.ShapeDtypeStruct((nb, s, d), F32),
        scratch_shapes=[pltpu.VMEM((d // LANES, tm + 2 * CONF_HALO, LANES), F32),
                        pltpu.VMEM((d // LANES, tm, LANES), F32)],
        compiler_params=_cparams(("parallel", "parallel")),
        name=name,
    )(z, z, z, x, mod, dw_w, row(dw_b), row(ln_g), row(ln_b), w_out, row(b_out))


def kernel(x, c, ctx, c_ctx, ada_w, ada_b, norm_mix_g, norm_ffn_g, attn_w_qkv, attn_w_o, attn_sink, lru_w_in, lru_conv_w, lru_conv_b, lru_wa, lru_ba, lru_wx, lru_bx, lru_lambda, lru_w_out, conf_w_in, conf_b_in, conf_dw_w, conf_dw_b, conf_ln_g, conf_ln_b, conf_w_out, conf_b_out, ffn_w_up, ffn_conv_w, ffn_conv_b, ffn_w_down, final_norm_g):
    bsz, seq, d = x.shape
    depth = ada_w.shape[0]
    assert bsz + 1 <= MOD_ROWS
    ctx_row = bsz
    bf = lambda w: w.astype(BF16)

    cvec = jnp.zeros((MOD_ROWS, d), F32).at[:bsz].set(c).at[ctx_row].set(c_ctx)
    mods = _ada_call(cvec, ada_w, ada_b)
    rope = _rope_tables(seq)
    q_cols = N_HEADS * HEAD_DIM
    k_cols = N_KV_HEADS * HEAD_DIM
    xc = ctx

    for i in range(depth):
        last = i == depth - 1
        kind, j = i % N_MIXERS, i // N_MIXERS
        mod = mods[i][:, None, :]
        g_mix = norm_mix_g[i]
        ctx_used = (not last) or kind != 2
        ffn_f32 = (ffn_w_up[i], ffn_w_down[i])
        if kind == 0:
            w_qkv, w_o = bf(attn_w_qkv[j]), bf(attn_w_o[j])
            qkv, ffn_mats = _proj_call(x, mod, None, g_mix, w_qkv, rope=rope, rope_cols=q_cols + k_cols, out_dtype=BF16,
                                       cast=ffn_f32, name="qkv")
            qkv_c, _ = _proj_call(xc, mod, ctx_row, g_mix, w_qkv, out_dtype=BF16, name="qkv_ctx")
            o = _attn_call(qkv, qkv_c, attn_sink[j])
            x = _oproj_call(o, w_o, x, mod, None, name="attn_out")
            if not last:
                o_c = _attn_ctx_call(qkv_c, attn_sink[j])
                xc = _oproj_call(o_c, w_o, xc, mod, ctx_row, name="attn_out_ctx")
        elif kind == 1:
            w_in, w_out = bf(lru_w_in[j]), bf(lru_w_out[j])
            u, _ = _proj_call(x, mod, None, g_mix, w_in, name="lru_in")
            u_c, _ = _proj_call(xc, mod, ctx_row, g_mix, w_in, name="lru_in_ctx")
            ffn_mats = [bf(a) for a in ffn_f32]
            z, z_c = _lru_call(u, u_c, lru_conv_w[j], lru_conv_b[j], bf(lru_wa[j]), lru_ba[j], bf(lru_wx[j]),
                               lru_bx[j], lru_lambda[j])
            x = _oproj_call(z, w_out, x, mod, None, name="lru_out")
            if not last:
                xc = _oproj_call(z_c, w_out, xc, mod, ctx_row, name="lru_out_ctx")
        else:
            w_in, w_out = bf(conf_w_in[j]), bf(conf_w_out[j])
            tail = (conf_dw_w[j], conf_dw_b[j], conf_ln_g[j], conf_ln_b[j], w_out, conf_b_out[j])
            zz, ffn_mats = _proj_call(x, mod, None, g_mix, w_in, conf_b_in[j], glu=True, cast=ffn_f32, name="conf_in")
            x = _conf_call(zz, x, mod, None, *tail, name="conf_tail")
            if ctx_used and not last:
                zz_c, _ = _proj_call(xc, mod, ctx_row, g_mix, w_in, conf_b_in[j], glu=True, name="conf_in_ctx")
                xc = _conf_call(zz_c, xc, mod, ctx_row, *tail, name="conf_tail_ctx")
        ffn_w = (norm_ffn_g[i], ffn_mats[0], ffn_conv_w[i], ffn_conv_b[i], ffn_mats[1])
        x = _ffn_call(x, mod, None, *ffn_w, final_g=final_norm_g if last else None, name="ffn")
        if not last:
            xc = _ffn_call(xc, mod, ctx_row, *ffn_w, name="ffn_ctx")
    return x
```

```python
import functools

import jax
import jax.numpy as jnp
from jax import lax
from jax.experimental import pallas as pl
from jax.experimental.pallas import tpu as pltpu

F32 = jnp.float32
BF16 = jnp.bfloat16

HEAD_DIM = 128
N_HEADS = 16
N_KV_HEADS = 4
GQA_GROUP = N_HEADS // N_KV_HEADS
ATT_BLOCK = 128
GRID_W = 64
ROPE_THETA = 10000.0
ROPE_FREQS = HEAD_DIM // 4
LRU_BLOCK = 128
LRU_C = 8.0
LRU_CONV = 4
CONF_WIDTH = 31
FFN_CONV = 3
NORM_EPS = 1e-6
NEG_INF = -1e30
N_MIXERS = 3

LANES = 128
SUBLANES = 8
ROW_STEP = 16
MOD_ROWS = 8
VMEM_LIMIT = 56 * 1024 * 1024


def _cparams(sem):
    return pltpu.CompilerParams(dimension_semantics=sem, vmem_limit_bytes=VMEM_LIMIT)


def _tile(n, pref):
    t = min(n, pref)
    assert n % t == 0, (n, t)
    return t


def _silu(x):
    return x * jax.nn.sigmoid(x)


def _rms_mod(x, g, shift, scale):
    ms = jnp.mean(x * x, axis=-1, keepdims=True)
    return (x * lax.rsqrt(ms + NORM_EPS) * g) * (1.0 + scale) + shift


def _mod_slice(mod_ref, k, d):
    return mod_ref[:, k * d:(k + 1) * d]


def _ada_kernel(c_ref, w_ref, b_ref, o_ref):
    s = _silu(c_ref[...]).astype(BF16)
    o_ref[...] = jnp.dot(s, w_ref[...].astype(BF16), preferred_element_type=F32) + b_ref[...]


def _ada_call(cvec, ada_w, ada_b):
    depth, d, n = ada_w.shape
    tn = _tile(n, 1024)
    return pl.pallas_call(
        _ada_kernel,
        grid=(depth, n // tn),
        in_specs=[pl.BlockSpec((MOD_ROWS, d), lambda l, j: (0, 0)),
                  pl.BlockSpec((None, d, tn), lambda l, j: (l, 0, j)),
                  pl.BlockSpec((None, 1, tn), lambda l, j: (l, 0, j))],
        out_specs=pl.BlockSpec((None, MOD_ROWS, tn), lambda l, j: (l, 0, j)),
        out_shape=jax.ShapeDtypeStruct((depth, MOD_ROWS, n), F32),
        compiler_params=_cparams(("parallel", "parallel")),
        name="ada",
    )(cvec, ada_w, ada_b.reshape(depth, 1, n))


def _rope(y, cos, sp, sm):
    return y * cos + pltpu.roll(y, 32, 1) * sp + pltpu.roll(y, HEAD_DIM - 32, 1) * sm


def _proj_kernel(*refs, d, tn, n_out, has_bias, glu, rope_cols, n_cast):
    it = iter(refs)
    x_ref, mod_ref, g_ref, w_ref = (next(it) for _ in range(4))
    b_ref = next(it) if has_bias else None
    rope_refs = [next(it) for _ in range(3 if rope_cols else 0)]
    cast_in = [next(it) for _ in range(n_cast)]
    o_ref = next(it)
    for src_ref in cast_in:
        next(it)[...] = src_ref[...].astype(BF16)
    h = _rms_mod(x_ref[...], g_ref[...], _mod_slice(mod_ref, 0, d), _mod_slice(mod_ref, 1, d)).astype(BF16)
    if rope_cols:
        cos, sp, sm = (r[...] for r in rope_refs)
    for c0 in range(0, n_out, tn):
        y = jnp.dot(h, w_ref[:, c0:c0 + tn], preferred_element_type=F32)
        if has_bias:
            y = y + b_ref[:, c0:c0 + tn]
        if glu:
            y2 = jnp.dot(h, w_ref[:, n_out + c0:n_out + c0 + tn], preferred_element_type=F32)
            if has_bias:
                y2 = y2 + b_ref[:, n_out + c0:n_out + c0 + tn]
            y = y * jax.nn.sigmoid(y2)
        if c0 < rope_cols:
            for c in range(0, tn, HEAD_DIM):
                o_ref[:, c0 + c:c0 + c + HEAD_DIM] = _rope(y[:, c:c + HEAD_DIM], cos, sp, sm).astype(o_ref.dtype)
        else:
            o_ref[:, c0:c0 + tn] = y.astype(o_ref.dtype)


CAST_BLOCKS = 32


def _proj_call(x, mod, mod_row, norm_g, w, bias=None, *, glu=False, rope=None, rope_cols=0,
               out_dtype=F32, tm_pref=512, tn_pref=512, cast=(), name="proj"):
    nb, s, d = x.shape
    n_w = w.shape[1]
    n_out = n_w // 2 if glu else n_w
    tm, tn = _tile(s, tm_pref), _tile(n_out, tn_pref)
    mod_map = (lambda b, i: (b, 0, 0)) if mod_row is None else (lambda b, i: (mod_row, 0, 0))
    const = lambda shape: pl.BlockSpec(shape, lambda b, i: (0, 0), pipeline_mode=pl.Buffered(1))
    args = [x, mod, norm_g.reshape(1, d), w]
    specs = [pl.BlockSpec((None, tm, d), lambda b, i: (b, i, 0)),
             pl.BlockSpec((None, 1, mod.shape[-1]), mod_map),
             const((1, d)), const((d, n_w))]
    if bias is not None:
        args.append(bias.reshape(1, n_w))
        specs.append(const((1, n_w)))
    if rope is not None:
        assert rope_cols % tn == 0
        for t in rope:
            args.append(t)
            specs.append(pl.BlockSpec((tm, HEAD_DIM), lambda b, i: (i, 0)))
    n_i = s // tm
    n_steps = nb * n_i
    row_tile = 2 * SUBLANES
    ride = (bool(cast) and n_steps % CAST_BLOCKS == 0
            and all(a.shape[1] % (CAST_BLOCKS * row_tile) == 0 for a, _ in cast))
    out_specs = [pl.BlockSpec((None, tm, n_out), lambda b, i: (b, i, 0))]
    out_shape = [jax.ShapeDtypeStruct((nb, s, n_out), out_dtype)]
    if ride:
        blk_of = lambda b, i: ((b * n_i + i) * CAST_BLOCKS) // n_steps
        for a, layer in cast:
            blk = (a.shape[1] // CAST_BLOCKS, a.shape[2])
            args.append(a)
            specs.append(pl.BlockSpec((None,) + blk, lambda b, i, layer=layer: (layer, blk_of(b, i), 0)))
            out_specs.append(pl.BlockSpec(blk, lambda b, i: (blk_of(b, i), 0)))
            out_shape.append(jax.ShapeDtypeStruct(a.shape[1:], BF16))
    kern = functools.partial(_proj_kernel, d=d, tn=tn, n_out=n_out, has_bias=bias is not None, glu=glu,
                             rope_cols=rope_cols if rope is not None else 0, n_cast=len(cast) if ride else 0)
    outs = pl.pallas_call(
        kern,
        grid=(nb, n_i),
        in_specs=specs,
        out_specs=out_specs,
        out_shape=out_shape,
        compiler_params=_cparams(("arbitrary", "arbitrary") if ride else ("parallel", "parallel")),
        name=name,
    )(*args)
    return outs[0], (list(outs[1:]) if ride else [a[layer].astype(BF16) for a, layer in cast])


def _rope_tables(s):
    t = jnp.arange(s)
    row = (t // GRID_W).astype(F32)
    col = (t % GRID_W).astype(F32)
    freq = ROPE_THETA ** (-jnp.arange(ROPE_FREQS, dtype=F32) / ROPE_FREQS)
    dd = jnp.arange(HEAD_DIM)
    axis, half, f = dd // (2 * ROPE_FREQS), (dd % (2 * ROPE_FREQS)) // ROPE_FREQS, dd % ROPE_FREQS
    pos = jnp.where(axis[None, :] == 0, row[:, None], col[:, None])
    ang = pos * freq[f][None, :]
    cos, sin = jnp.cos(ang), jnp.sin(ang)
    sp = jnp.where(half[None, :] == 1, sin, 0.0)
    sm = jnp.where(half[None, :] == 0, -sin, 0.0)
    return cos, sp, sm


def _fold_lanes(parts, op):
    tiles = [p[:, j:j + LANES] for p in parts for j in range(0, p.shape[1], LANES)]
    acc = tiles[0]
    for t in tiles[1:]:
        acc = op(acc, t)
    return acc


ATT_HEADS_PER_UNIT = 4


def _attn_heads(q_ref, k_parts, v_parts, sink_ref, o_ref, masks):
    tq = q_ref.shape[0]
    hpu = ATT_HEADS_PER_UNIT
    log2e = 1.4426950408889634
    c = (HEAD_DIM ** -0.5) * log2e
    grp = lax.broadcasted_iota(jnp.int32, (hpu * tq, 1), 0) // tq
    units = [(h0 // GQA_GROUP, h0) for h0 in range(0, N_HEADS, hpu)]

    def scores(kh, h0):
        ksl = slice(kh * HEAD_DIM, (kh + 1) * HEAD_DIM)
        qg = jnp.concatenate([q_ref[:, (h0 + g) * HEAD_DIM:(h0 + g + 1) * HEAD_DIM] for g in range(hpu)], axis=0)
        out = []
        for r, mask in zip(k_parts, masks):
            s = lax.dot_general(qg, r[:, ksl], (((1,), (1,)), ((), ())), preferred_element_type=F32) * c
            out.append(s if mask is None else jnp.where(mask, s, NEG_INF))
        return out

    def finish(kh, h0, ps, l):
        ksl = slice(kh * HEAD_DIM, (kh + 1) * HEAD_DIM)
        o = None
        for p, r in zip(ps, v_parts):
            pv = jnp.dot(p.astype(BF16), r[:, ksl], preferred_element_type=F32)
            o = pv if o is None else o + pv
        o = o / l
        for g in range(hpu):
            o_ref[:, (h0 + g) * HEAD_DIM:(h0 + g + 1) * HEAD_DIM] = o[g * tq:(g + 1) * tq].astype(o_ref.dtype)

    nxt = scores(*units[0])
    pending = None
    for u, (kh, h0) in enumerate(units):
        ss = nxt
        if u + 1 < len(units):
            nxt = scores(*units[u + 1])
        if pending is not None:
            finish(*pending)
        sink = jnp.full((hpu * tq, 1), sink_ref[h0], F32)
        for g in range(1, hpu):
            sink = jnp.where(grp == g, sink_ref[h0 + g], sink)
        sink = sink * log2e
        m = jnp.maximum(jnp.max(_fold_lanes(ss, jnp.maximum), axis=-1, keepdims=True), sink)
        ps = [jnp.exp2(s - m) for s in ss]
        l = jnp.sum(_fold_lanes(ps, jnp.add), axis=-1, keepdims=True) + jnp.exp2(sink - m)
        pending = (kh, h0, ps, l)
    finish(*pending)


def _attn_kernel(sink_ref, q_ref, kp_ref, kc_ref, kn_ref, vp_ref, vc_ref, vn_ref, kx_ref, vx_ref, o_ref):
    n = pl.program_id(1)
    nblk = pl.num_programs(1)
    tq = ATT_BLOCK
    r = lax.broadcasted_iota(jnp.int32, (ATT_HEADS_PER_UNIT * tq, tq), 0) % tq
    col = lax.broadcasted_iota(jnp.int32, (ATT_HEADS_PER_UNIT * tq, tq), 1)
    mask_prev = (col >= r) & (n > 0)
    mask_next = (col <= r) & (n < nblk - 1)
    _attn_heads(q_ref, [kp_ref, kc_ref, kn_ref, kx_ref], [vp_ref, vc_ref, vn_ref, vx_ref], sink_ref, o_ref,
                [mask_prev, None, mask_next, None])


def _attn_call(qkv, qkv_ctx, sink):
    b, s, _ = qkv.shape
    n_ctx = qkv_ctx.shape[1]
    tq = ATT_BLOCK
    nblk = s // tq
    qc, kc = N_HEADS * HEAD_DIM, N_KV_HEADS * HEAD_DIM
    kblk, vblk = qc // kc, qc // kc + 1
    kv_spec = lambda off, cb: pl.BlockSpec(
        (None, tq, kc), lambda bb, n: (bb, jnp.clip(n + off, 0, nblk - 1), cb))
    return pl.pallas_call(
        _attn_kernel,
        grid=(b, nblk),
        in_specs=[pl.BlockSpec(memory_space=pltpu.SMEM),
                  pl.BlockSpec((None, tq, qc), lambda bb, n: (bb, n, 0)),
                  kv_spec(-1, kblk), kv_spec(0, kblk), kv_spec(1, kblk),
                  kv_spec(-1, vblk), kv_spec(0, vblk), kv_spec(1, vblk),
                  pl.BlockSpec((None, n_ctx, kc), lambda bb, n: (bb, 0, kblk)),
                  pl.BlockSpec((None, n_ctx, kc), lambda bb, n: (bb, 0, vblk))],
        out_specs=pl.BlockSpec((None, tq, qc), lambda bb, n: (bb, n, 0)),
        out_shape=jax.ShapeDtypeStruct((b, s, qc), BF16),
        compiler_params=_cparams(("parallel", "parallel")),
        name="attn",
    )(sink, qkv, qkv, qkv, qkv, qkv, qkv, qkv, qkv_ctx, qkv_ctx)


def _attn_ctx_kernel(sink_ref, q_ref, kx_ref, vx_ref, o_ref):
    _attn_heads(q_ref, [kx_ref], [vx_ref], sink_ref, o_ref, [None])


def _attn_ctx_call(qkv_ctx, sink):
    b, n_ctx, _ = qkv_ctx.shape
    qc, kc = N_HEADS * HEAD_DIM, N_KV_HEADS * HEAD_DIM
    tq = _tile(n_ctx, ATT_BLOCK)
    return pl.pallas_call(
        _attn_ctx_kernel,
        grid=(b, n_ctx // tq),
        in_specs=[pl.BlockSpec(memory_space=pltpu.SMEM),
                  pl.BlockSpec((None, tq, qc), lambda bb, n: (bb, n, 0)),
                  pl.BlockSpec((None, n_ctx, kc), lambda bb, n: (bb, 0, qc // kc)),
                  pl.BlockSpec((None, n_ctx, kc), lambda bb, n: (bb, 0, qc // kc + 1))],
        out_specs=pl.BlockSpec((None, tq, qc), lambda bb, n: (bb, n, 0)),
        out_shape=jax.ShapeDtypeStruct((b, n_ctx, qc), BF16),
        compiler_params=_cparams(("parallel", "parallel")),
        name="attn_ctx",
    )(sink, qkv_ctx, qkv_ctx, qkv_ctx)


def _oproj_kernel(a_ref, w_ref, x_ref, mod_ref, o_ref, *, d, gate_idx):
    y = jnp.dot(a_ref[...], w_ref[...], preferred_element_type=F32)
    o_ref[...] = x_ref[...] + _mod_slice(mod_ref, gate_idx, d) * y


def _oproj_call(a, w, x, mod, mod_row, *, gate_idx=2, tm_pref=512, name="oproj"):
    nb, s, d = x.shape
    k = a.shape[-1]
    tm = _tile(s, tm_pref)
    mod_map = (lambda b, i: (b, 0, 0)) if mod_row is None else (lambda b, i: (mod_row, 0, 0))
    return pl.pallas_call(
        functools.partial(_oproj_kernel, d=d, gate_idx=gate_idx),
        grid=(nb, s // tm),
        in_specs=[pl.BlockSpec((None, tm, k), lambda b, i: (b, i, 0)),
                  pl.BlockSpec((k, d), lambda b, i: (0, 0)),
                  pl.BlockSpec((None, tm, d), lambda b, i: (b, i, 0)),
                  pl.BlockSpec((None, 1, mod.shape[-1]), mod_map)],
        out_specs=pl.BlockSpec((None, tm, d), lambda b, i: (b, i, 0)),
        out_shape=jax.ShapeDtypeStruct((nb, s, d), F32),
        compiler_params=_cparams(("parallel", "parallel")),
        name=name,
    )(a, w, x, mod)


FFN_CHUNK = 1024
FFN_SUB = 512


def _ffn_kernel(*refs, d, tm, ff, n_i, seq, final_norm):
    it = iter(refs)
    xp_ref, x_ref, xn_ref, mod_ref, g_ref, cw_ref, cb_ref = (next(it) for _ in range(7))
    fg_ref = next(it) if final_norm else None
    wup_hbm, wdn_hbm, o_ref = next(it), next(it), next(it)
    h_ref, acc_ref, wg_buf, wv_buf, wd_buf, sem = (next(it) for _ in range(6))
    t, n_t = pl.program_id(0), pl.num_programs(0)
    i = t % n_i
    halo = SUBLANES
    n_full, tail = ff // FFN_CHUNK, ff % FFN_CHUNK
    chunks = [(k * FFN_CHUNK, FFN_CHUNK) for k in range(n_full)] + ([(n_full * FFN_CHUNK, tail)] if tail else [])
    n_chunks = len(chunks)
    cross_tile = n_chunks % 2 == 0

    def copies(c0, width, slot):
        return (pltpu.make_async_copy(wup_hbm.at[:, pl.ds(c0, width)], wg_buf.at[slot, :, pl.ds(0, width)],
                                      sem.at[slot, 0]),
                pltpu.make_async_copy(wup_hbm.at[:, pl.ds(ff + c0, width)], wv_buf.at[slot, :, pl.ds(0, width)],
                                      sem.at[slot, 1]),
                pltpu.make_async_copy(wdn_hbm.at[pl.ds(c0, width), :], wd_buf.at[slot, pl.ds(0, width), :],
                                      sem.at[slot, 2]))

    def start(c0, width, slot):
        for cp in copies(c0, width, slot):
            cp.start()

    def wait(c0, width, slot):
        for cp in copies(c0, width, slot):
            cp.wait()

    if cross_tile:
        @pl.when(t == 0)
        def _():
            start(*chunks[0], 0)
    else:
        start(*chunks[0], 0)

    rows = lambda v: jnp.broadcast_to(v, (ROW_STEP, d))
    g, shift, scale = rows(g_ref[...]), rows(_mod_slice(mod_ref, 3, d)), rows(_mod_slice(mod_ref, 4, d))
    for e0 in range(0, tm + 2 * halo, ROW_STEP):
        if e0 == 0:
            xs = jnp.concatenate([xp_ref[...], x_ref[0:ROW_STEP - halo]], axis=0)
        elif e0 + ROW_STEP > tm + halo:
            xs = jnp.concatenate([x_ref[e0 - halo:tm], xn_ref[...]], axis=0)
        else:
            xs = x_ref[e0 - halo:e0 - halo + ROW_STEP]
        h_ref[e0:e0 + ROW_STEP] = _rms_mod(xs, g, shift, scale).astype(BF16)
    row = lax.broadcasted_iota(jnp.int32, (tm + 2 * halo, 1), 0)
    keep = ((row >= halo) | (i > 0)) & ((row < tm + halo) | (i < n_i - 1))
    pos = lax.broadcasted_iota(jnp.int32, (tm, 1), 0) % seq
    tap_ok = [None if seq >= tm or k == FFN_CONV // 2 else
              (pos + (k - FFN_CONV // 2) >= 0) & (pos + (k - FFN_CONV // 2) < seq) for k in range(FFN_CONV)]
    gate2 = rows(_mod_slice(mod_ref, 5, d))
    fg = rows(fg_ref[...]) if final_norm else None

    def compute(c0, width, slot, first, last):
        h = h_ref[...]
        for j in range(0, width, FFN_SUB):
            gate = jnp.dot(h, wg_buf[slot, :, j:j + FFN_SUB], preferred_element_type=F32)
            val = jnp.dot(h, wv_buf[slot, :, j:j + FFN_SUB], preferred_element_type=F32)[halo:halo + tm]
            gate = jnp.where(keep, gate, 0.0)
            cols = pl.ds(pl.multiple_of(c0 + j, FFN_SUB), FFN_SUB)
            cw = cw_ref[:, cols]
            gc = cb_ref[:, cols]
            for k in range(FFN_CONV):
                off = halo + k - FFN_CONV // 2
                gk = gate[off:off + tm]
                if tap_ok[k] is not None:
                    gk = jnp.where(tap_ok[k], gk, 0.0)
                gc = gc + cw[k:k + 1] * gk
            act = (_silu(gc) * val).astype(BF16)
            part = jnp.dot(act, wd_buf[slot, j:j + FFN_SUB, :], preferred_element_type=F32)
            if first and j == 0:
                acc_ref[...] = part
            elif last and j + FFN_SUB >= width:
                for r0 in range(0, tm, ROW_STEP):
                    out = x_ref[r0:r0 + ROW_STEP] + gate2 * (acc_ref[r0:r0 + ROW_STEP] + part[r0:r0 + ROW_STEP])
                    if final_norm:
                        ms = jnp.mean(out * out, axis=-1, keepdims=True)
                        out = out * lax.rsqrt(ms + NORM_EPS) * fg
                    o_ref[r0:r0 + ROW_STEP] = out
            else:
                acc_ref[...] += part

    def chunk(k, static):
        slot = k % 2
        if static:
            if k + 1 < n_chunks:
                start(*chunks[k + 1], 1 - slot)
            elif cross_tile:
                @pl.when(t + 1 < n_t)
                def _():
                    start(*chunks[0], 0)
            wait(*chunks[k], slot)
            compute(*chunks[k], slot, k == 0, k == n_chunks - 1)
        else:
            start((k + 1) * FFN_CHUNK, FFN_CHUNK, 1 - slot)
            wait(k * FFN_CHUNK, FFN_CHUNK, slot)
            compute(k * FFN_CHUNK, FFN_CHUNK, slot, False, False)

    head = 1
    tail_static = min(2, n_chunks - head)
    chunk(0, True)
    lax.fori_loop(head, n_chunks - tail_static, lambda k, c: (chunk(k, False), c)[1], 0)
    for k in range(n_chunks - tail_static, n_chunks):
        chunk(k, True)


def _ffn_call(x, mod, mod_row, norm_g, w_up, conv_w, conv_b, w_down, final_g=None, *, tm_pref=512, name="ffn"):
    out_shape = x.shape
    seq = x.shape[1]
    if mod_row is not None and x.shape[0] * seq <= tm_pref:
        x = x.reshape(1, x.shape[0] * seq, x.shape[2])
    nb, s, d = x.shape
    ff = w_down.shape[0]
    assert ff % FFN_SUB == 0
    tm = _tile(s, tm_pref)
    assert tm % ROW_STEP == 0 and ROW_STEP == 2 * SUBLANES and (seq >= tm or tm % seq == 0)
    assert ff > FFN_CHUNK
    n_i = s // tm
    r8 = tm // SUBLANES
    bi = lambda t: (t // n_i, t % n_i)
    mod_map = (lambda t: (t // n_i, 0, 0)) if mod_row is None else (lambda t: (mod_row, 0, 0))
    const = lambda shape: pl.BlockSpec(shape, lambda t: (0, 0), pipeline_mode=pl.Buffered(1))
    args = [x, x, x, mod, norm_g.reshape(1, d), conv_w, conv_b.reshape(1, ff)]
    specs = [pl.BlockSpec((None, SUBLANES, d), lambda t: (bi(t)[0], jnp.maximum(bi(t)[1] * r8 - 1, 0), 0)),
             pl.BlockSpec((None, tm, d), lambda t: (bi(t)[0], bi(t)[1], 0)),
             pl.BlockSpec((None, SUBLANES, d),
                          lambda t: (bi(t)[0], jnp.minimum((bi(t)[1] + 1) * r8, s // SUBLANES - 1), 0)),
             pl.BlockSpec((None, 1, mod.shape[-1]), mod_map),
             const((1, d)), const((FFN_CONV, ff)), const((1, ff))]
    if final_g is not None:
        args.append(final_g.reshape(1, d))
        specs.append(const((1, d)))
    args += [w_up, w_down]
    specs += [pl.BlockSpec(memory_space=pl.ANY), pl.BlockSpec(memory_space=pl.ANY)]
    cw = min(FFN_CHUNK, ff)
    return pl.pallas_call(
        functools.partial(_ffn_kernel, d=d, tm=tm, ff=ff, n_i=n_i, seq=seq, final_norm=final_g is not None),
        grid=(nb * n_i,),
        in_specs=specs,
        out_specs=pl.BlockSpec((None, tm, d), lambda t: (bi(t)[0], bi(t)[1], 0)),
        out_shape=jax.ShapeDtypeStruct((nb, s, d), F32),
        scratch_shapes=[pltpu.VMEM((tm + 2 * SUBLANES, d), BF16), pltpu.VMEM((tm, d), F32),
                        pltpu.VMEM((2, d, cw), BF16), pltpu.VMEM((2, d, cw), BF16), pltpu.VMEM((2, cw, d), BF16),
                        pltpu.SemaphoreType.DMA((2, 3))],
        compiler_params=_cparams(("arbitrary",)),
        name=name,
    )(*args).reshape(out_shape)


LRU_SEG = SUBLANES
LRU_CH = 32


def _lru_pitch(n):
    p = n // LRU_SEG + SUBLANES
    return p if (p // SUBLANES) % 2 == 1 else p + SUBLANES


def _fill_pitched(xp_ref, x_ref, n):
    seg, pitch = n // LRU_SEG, _lru_pitch(n)
    lanes = x_ref.shape[1]
    zero = jnp.zeros((SUBLANES, lanes), F32)
    row = lax.broadcasted_iota(jnp.int32, (SUBLANES, lanes), 0)
    xp_ref[0:SUBLANES] = zero
    for j in range(LRU_SEG):
        base = SUBLANES + j * pitch
        xp_ref[base:base + seg] = x_ref[j * seg:(j + 1) * seg]
        tail = x_ref[(j + 1) * seg - SUBLANES:(j + 1) * seg]
        head = x_ref[(j + 1) * seg:(j + 1) * seg + SUBLANES] if j + 1 < LRU_SEG else zero
        if pitch == seg + SUBLANES:
            xp_ref[base + seg:base + pitch] = jnp.where(row < SUBLANES // 2, head, tail)
        else:
            xp_ref[base + seg:base + seg + SUBLANES] = head
            xp_ref[base + pitch - SUBLANES:base + pitch] = tail


def _seg_rows(step, pitch):
    return pl.ds(SUBLANES + step, LRU_SEG, stride=pitch)


def _lru_chunk(xp_ref, hs_ref, ac_ref, pitch, s0, rev, par, h, ac):
    cwb, cbb, wa, ba, wx, bx, kq = par
    lo = 0 if rev else -(LRU_CONV - 1)
    xs = [xp_ref[_seg_rows(s0 + lo + i, pitch), :] for i in range(LRU_CH + LRU_CONV - 1)]
    ucs = []
    for s in range(LRU_CH):
        u = cbb
        for k in range(LRU_CONV):
            u = u + cwb[k] * xs[s + k]
        ucs.append(u)
    uc = jnp.concatenate(ucs, axis=0)
    ucb = uc.astype(BF16)
    ta = jnp.tanh(0.5 * (jnp.dot(ucb, wa, preferred_element_type=F32) + ba))
    tx = jnp.tanh(0.5 * (jnp.dot(ucb, wx, preferred_element_type=F32) + bx))
    t = jnp.tanh(kq + kq * ta)
    q = 1.0 / (1.0 - t)
    a = (1.0 + t) * q
    bt = (2.0 * jnp.sqrt(-t) * q) * ((0.5 + 0.5 * tx) * uc)
    hs, acs = [None] * LRU_CH, [None] * LRU_CH
    for s in (range(LRU_CH - 1, -1, -1) if rev else range(LRU_CH)):
        a_s = a[s * LRU_SEG:(s + 1) * LRU_SEG]
        h = a_s * h + bt[s * LRU_SEG:(s + 1) * LRU_SEG]
        ac = a_s * ac
        hs[s], acs[s] = h, ac
    r0 = pl.multiple_of(s0 * LRU_SEG, LRU_CH * LRU_SEG)
    hs_ref[pl.ds(r0, LRU_CH * LRU_SEG), :] = jnp.concatenate(hs, axis=0)
    ac_ref[pl.ds(r0, LRU_CH * LRU_SEG), :] = jnp.concatenate(acs, axis=0)
    return h, ac


def _lru_pass(xp_ref, scr, n, pars):
    seg, pitch = n // LRU_SEG, _lru_pitch(n)
    nch = seg // LRU_CH
    lanes = xp_ref.shape[1]

    def body(ci, carry):
        hf, af, hr, ar = carry
        hf, af = _lru_chunk(xp_ref, scr[0], scr[1], pitch, ci * LRU_CH, False, pars[0], hf, af)
        hr, ar = _lru_chunk(xp_ref, scr[2], scr[3], pitch, (nch - 1 - ci) * LRU_CH, True, pars[1], hr, ar)
        return hf, af, hr, ar

    zero, one = jnp.zeros((LRU_SEG, lanes), F32), jnp.ones((LRU_SEG, lanes), F32)
    hf, af, hr, ar = lax.fori_loop(0, nch, body, (zero, one, zero, one))
    return (hf, af), (hr, ar)


def _segment_carries(h_end, a_end, h_in, rev):
    row = lax.broadcasted_iota(jnp.int32, h_end.shape, 0)
    sh = LRU_SEG - 1 if rev else 1
    ph, pa = pltpu.roll(h_end, sh, 0), pltpu.roll(a_end, sh, 0)
    c = h_in
    for j in (range(LRU_SEG - 2, -1, -1) if rev else range(1, LRU_SEG)):
        c = jnp.where(row == j, ph + pa * pltpu.roll(c, sh, 0), c)
    last = 0 if rev else LRU_SEG - 1
    end = (h_end + a_end * c)[last:last + 1]
    return c, jnp.broadcast_to(end, h_end.shape)


def _lru_finish(hs_f, ac_f, hs_r, ac_r, c_f, c_r, xp_ref, g_ref, z_ref, n):
    seg, pitch = n // LRU_SEG, _lru_pitch(n)
    nch = seg // LRU_CH
    rows = LRU_CH * LRU_SEG
    cf = jnp.concatenate([c_f] * LRU_CH, axis=0)
    cr = jnp.concatenate([c_r] * LRU_CH, axis=0)

    def comb(ci, carry):
        r0 = pl.multiple_of(ci * rows, rows)
        sl = pl.ds(r0, rows)
        tot = (hs_f[sl, :] + ac_f[sl, :] * cf) + (hs_r[sl, :] + ac_r[sl, :] * cr)
        for s in range(LRU_CH):
            xp_ref[_seg_rows(ci * LRU_CH + s, pitch), :] = tot[s * LRU_SEG:(s + 1) * LRU_SEG]
        return carry

    lax.fori_loop(0, nch, comb, 0)
    rt = min(seg, 128)
    for j in range(LRU_SEG):
        def gate(ri, carry, j=j):
            r = pl.multiple_of(ri * rt, rt)
            tot = xp_ref[pl.ds(SUBLANES + j * pitch + r, rt), :]
            z_ref[pl.ds(j * seg + r, rt), :] = (jax.nn.gelu(g_ref[pl.ds(j * seg + r, rt), :]) * tot).astype(z_ref.dtype)
            return carry

        lax.fori_loop(0, seg // rt, gate, 0)


def _lru_kernel(gl_ref, xl_ref, gc_ref, xc_ref, cw_ref, cb_ref, wa_ref, ba_ref, wx_ref, bx_ref, lam_ref,
                zl_ref, zc_ref, xpl_ref, xpc_ref, *scr):
    s, n_ctx = xl_ref.shape[0], xc_ref.shape[0]
    lanes = xl_ref.shape[1]
    lat, ctx = scr[:4], scr[4:]
    _fill_pitched(xpc_ref, xc_ref, n_ctx)
    _fill_pitched(xpl_ref, xl_ref, s)
    pars = []
    for dr in range(2):
        nl = -lam_ref[dr]
        sp = jnp.maximum(nl, 0.0) + jnp.log1p(jnp.exp(-jnp.abs(nl)))
        cw = cw_ref[dr]
        cwb = [jnp.broadcast_to(cw[k:k + 1], (LRU_SEG, lanes)) for k in range(LRU_CONV)]
        pars.append((cwb, jnp.broadcast_to(cb_ref[dr], (LRU_SEG, lanes)), wa_ref[dr], ba_ref[dr], wx_ref[dr],
                     bx_ref[dr], (-0.25 * LRU_C) * sp))
    zero = jnp.zeros((LRU_SEG, lanes), F32)
    ends_c = _lru_pass(xpc_ref, ctx, n_ctx, pars)
    ends_l = _lru_pass(xpl_ref, lat, s, pars)
    c_ctx, c_lat = [], []
    for dr in range(2):
        c, state = _segment_carries(*ends_c[dr], zero, dr == 1)
        c_ctx.append(c)
        c_lat.append(_segment_carries(*ends_l[dr], state, dr == 1)[0])
    _lru_finish(*ctx, *c_ctx, xpc_ref, gc_ref, zc_ref, n_ctx)
    _lru_finish(*lat, *c_lat, xpl_ref, gl_ref, zl_ref, s)


def _lru_call(u, u_ctx, conv_w, conv_b, wa, ba, wx, bx, lam):
    b, s, r2 = u.shape
    r = r2 // 2
    n_ctx = u_ctx.shape[1]
    lanes = LRU_BLOCK
    nblk = r // lanes
    for n in (s, n_ctx):
        assert n % (LRU_SEG * LRU_CH) == 0, n
    vec = lambda v: v.reshape(2, 1, r)
    col = lambda n, off: pl.BlockSpec((None, n, lanes), lambda bb, c: (bb, 0, c + off))
    par = lambda rows: pl.BlockSpec((2, rows, lanes), lambda bb, c: (0, 0, c))
    wsp = pl.BlockSpec((2, None, lanes, lanes), lambda bb, c: (0, c, 0, 0))
    pitched = lambda n: pltpu.VMEM((SUBLANES + LRU_SEG * _lru_pitch(n), lanes), F32)
    return pl.pallas_call(
        _lru_kernel,
        grid=(b, nblk),
        in_specs=[col(s, 0), col(s, nblk), col(n_ctx, 0), col(n_ctx, nblk),
                  par(LRU_CONV), par(1), wsp, par(1), wsp, par(1), par(1)],
        out_specs=[col(s, 0), col(n_ctx, 0)],
        out_shape=[jax.ShapeDtypeStruct((b, s, r), BF16), jax.ShapeDtypeStruct((b, n_ctx, r), BF16)],
        scratch_shapes=[pitched(s), pitched(n_ctx)] + [pltpu.VMEM((s, lanes), F32)] * 4
                       + [pltpu.VMEM((n_ctx, lanes), F32)] * 4,
        compiler_params=_cparams(("parallel", "parallel")),
        name="lru",
    )(u, u, u_ctx, u_ctx, conv_w, vec(conv_b), wa, vec(ba), wx, vec(bx), vec(lam))


CONF_HALO = 16
CONF_ROWS = 64
CONF_PARTS = 2


def _conf_kernel(zp_ref, z_ref, zn_ref, x_ref, mod_ref, dw_ref, db_ref, lg_ref, lb_ref, wo_ref, bo_ref,
                 o_ref, zw_ref, y_ref, *, d, tm):
    i, n_i = pl.program_id(1), pl.num_programs(1)
    n_slab = d // LANES
    zprev = jnp.where(i > 0, zp_ref[...], 0.0)
    znext = jnp.where(i < n_i - 1, zn_ref[...], 0.0)
    for c in range(n_slab):
        sl = slice(c * LANES, (c + 1) * LANES)
        zw_ref[c, 0:CONF_HALO] = zprev[:, sl]
        zw_ref[c, CONF_HALO:CONF_HALO + tm] = z_ref[:, sl]
        zw_ref[c, CONF_HALO + tm:2 * CONF_HALO + tm] = znext[:, sl]
    base = CONF_HALO - CONF_WIDTH // 2
    rows = min(CONF_ROWS, tm)
    pair = 2 * SUBLANES
    n_ld = rows + pair

    def slab(c, carry):
        l0 = pl.multiple_of(c * LANES, LANES)
        w = dw_ref[:, pl.ds(l0, LANES)]
        wb = [jnp.broadcast_to(w[k:k + 1], (SUBLANES, LANES)) for k in range(CONF_WIDTH)]
        bias = jnp.broadcast_to(db_ref[:, pl.ds(l0, LANES)], (SUBLANES, LANES))

        def block(rb, carry2):
            r0 = rb * rows
            z2 = [zw_ref[c, pl.ds(r0 + base + a, SUBLANES, stride=2), :] for a in range(n_ld)]
            for p in range(rows // pair):
                for odd in range(2):
                    parts = [None] * CONF_PARTS
                    for k in range(CONF_WIDTH):
                        term = wb[k] * z2[p * pair + odd + k]
                        q = k % CONF_PARTS
                        parts[q] = term if parts[q] is None else parts[q] + term
                    acc = bias
                    for q in range(CONF_PARTS):
                        acc = acc + parts[q]
                    y_ref[c, pl.ds(r0 + p * pair + odd, SUBLANES, stride=2), :] = acc
            return carry2

        lax.fori_loop(0, tm // rows, block, 0)
        return carry

    lax.fori_loop(0, n_slab, slab, 0)
    y = jnp.concatenate([y_ref[c] for c in range(n_slab)], axis=1)
    mu = jnp.mean(y, axis=-1, keepdims=True)
    yc = y - mu
    var = jnp.mean(yc * yc, axis=-1, keepdims=True)
    y = _silu(yc * lax.rsqrt(var + NORM_EPS) * lg_ref[...] + lb_ref[...])
    out = jnp.dot(y.astype(BF16), wo_ref[...], preferred_element_type=F32) + bo_ref[...]
    o_ref[...] = x_ref[...] + _mod_slice(mod_ref, 2, d) * out


def _conf_call(z, x, mod, mod_row, dw_w, dw_b, ln_g, ln_b, w_out, b_out, *, tm_pref=512, name="conf"):
    nb, s, d = x.shape
    tm = _tile(s, tm_pref)
    rh = tm // CONF_HALO
    row = lambda v: v.reshape(1, d)
    mod_map = (lambda b, i: (b, 0, 0)) if mod_row is None else (lambda b, i: (mod_row, 0, 0))
    const = lambda shape: pl.BlockSpec(shape, lambda b, i: (0, 0), pipeline_mode=pl.Buffered(1))
    return pl.pallas_call(
        functools.partial(_conf_kernel, d=d, tm=tm),
        grid=(nb, s // tm),
        in_specs=[pl.BlockSpec((None, CONF_HALO, d), lambda b, i: (b, jnp.maximum(i * rh - 1, 0), 0)),
                  pl.BlockSpec((None, tm, d), lambda b, i: (b, i, 0)),
                  pl.BlockSpec((None, CONF_HALO, d),
                               lambda b, i: (b, jnp.minimum((i + 1) * rh, s // CONF_HALO - 1), 0)),
                  pl.BlockSpec((None, tm, d), lambda b, i: (b, i, 0)),
                  pl.BlockSpec((None, 1, mod.shape[-1]), mod_map),
                  const((CONF_WIDTH, d)), const((1, d)), const((1, d)), const((1, d)),
                  const((d, d)), const((1, d))],
        out_specs=pl.BlockSpec((None, tm, d), lambda b, i: (b, i, 0)),
        out_shape=jax.ShapeDtypeStruct((nb, s, d), F32),
        scratch_shapes=[pltpu.VMEM((d // LANES, tm + 2 * CONF_HALO, LANES), F32),
                        pltpu.VMEM((d // LANES, tm, LANES), F32)],
        compiler_params=_cparams(("parallel", "parallel")),
        name=name,
    )(z, z, z, x, mod, dw_w, row(dw_b), row(ln_g), row(ln_b), w_out, row(b_out))


def kernel(x, c, ctx, c_ctx, ada_w, ada_b, norm_mix_g, norm_ffn_g, attn_w_qkv, attn_w_o, attn_sink, lru_w_in, lru_conv_w, lru_conv_b, lru_wa, lru_ba, lru_wx, lru_bx, lru_lambda, lru_w_out, conf_w_in, conf_b_in, conf_dw_w, conf_dw_b, conf_ln_g, conf_ln_b, conf_w_out, conf_b_out, ffn_w_up, ffn_conv_w, ffn_conv_b, ffn_w_down, final_norm_g):
    bsz, seq, d = x.shape
    depth = ada_w.shape[0]
    assert bsz + 1 <= MOD_ROWS
    ctx_row = bsz
    bf = lambda w: w.astype(BF16)

    cvec = jnp.zeros((MOD_ROWS, d), F32).at[:bsz].set(c).at[ctx_row].set(c_ctx)
    mods = _ada_call(cvec, ada_w, ada_b)
    rope = _rope_tables(seq)
    q_cols = N_HEADS * HEAD_DIM
    k_cols = N_KV_HEADS * HEAD_DIM
    xc = ctx

    for i in range(depth):
        last = i == depth - 1
        kind, j = i % N_MIXERS, i // N_MIXERS
        mod = mods[i][:, None, :]
        g_mix = norm_mix_g[i]
        ctx_used = (not last) or kind != 2
        ffn_f32 = ((ffn_w_up, i), (ffn_w_down, i))
        if kind == 0:
            w_qkv, w_o = bf(attn_w_qkv[j]), bf(attn_w_o[j])
            qkv, ffn_mats = _proj_call(x, mod, None, g_mix, w_qkv, rope=rope, rope_cols=q_cols + k_cols, out_dtype=BF16,
                                       cast=ffn_f32, name="qkv")
            qkv_c, _ = _proj_call(xc, mod, ctx_row, g_mix, w_qkv, out_dtype=BF16, name="qkv_ctx")
            o = _attn_call(qkv, qkv_c, attn_sink[j])
            x = _oproj_call(o, w_o, x, mod, None, name="attn_out")
            if not last:
                o_c = _attn_ctx_call(qkv_c, attn_sink[j])
                xc = _oproj_call(o_c, w_o, xc, mod, ctx_row, name="attn_out_ctx")
        elif kind == 1:
            w_in, w_out = bf(lru_w_in[j]), bf(lru_w_out[j])
            u, _ = _proj_call(x, mod, None, g_mix, w_in, name="lru_in")
            u_c, _ = _proj_call(xc, mod, ctx_row, g_mix, w_in, name="lru_in_ctx")
            ffn_mats = [bf(a[i]) for a, _ in ffn_f32]
            z, z_c = _lru_call(u, u_c, lru_conv_w[j], lru_conv_b[j], bf(lru_wa[j]), lru_ba[j], bf(lru_wx[j]),
                               lru_bx[j], lru_lambda[j])
            x = _oproj_call(z, w_out, x, mod, None, name="lru_out")
            if not last:
                xc = _oproj_call(z_c, w_out, xc, mod, ctx_row, name="lru_out_ctx")
        else:
            w_in, w_out = bf(conf_w_in[j]), bf(conf_w_out[j])
            tail = (conf_dw_w[j], conf_dw_b[j], conf_ln_g[j], conf_ln_b[j], w_out, conf_b_out[j])
            zz, ffn_mats = _proj_call(x, mod, None, g_mix, w_in, conf_b_in[j], glu=True, cast=ffn_f32, name="conf_in")
            x = _conf_call(zz, x, mod, None, *tail, name="conf_tail")
            if ctx_used and not last:
                zz_c, _ = _proj_call(xc, mod, ctx_row, g_mix, w_in, conf_b_in[j], glu=True, name="conf_in_ctx")
                xc = _conf_call(zz_c, xc, mod, ctx_row, *tail, name="conf_tail_ctx")
        ffn_w = (norm_ffn_g[i], ffn_mats[0], ffn_conv_w[i], ffn_conv_b[i], ffn_mats[1])
        x = _ffn_call(x, mod, None, *ffn_w, final_g=final_norm_g if last else None, name="ffn")
        if not last:
            xc = _ffn_call(xc, mod, ctx_row, *ffn_w, name="ffn_ctx")
    return x
```

```python
import functools

import jax
import jax.numpy as jnp
from jax import lax
from jax.experimental import pallas as pl
from jax.experimental.pallas import tpu as pltpu

F32 = jnp.float32
BF16 = jnp.bfloat16

HEAD_DIM = 128
N_HEADS = 16
N_KV_HEADS = 4
GQA_GROUP = N_HEADS // N_KV_HEADS
ATT_BLOCK = 128
GRID_W = 64
ROPE_THETA = 10000.0
ROPE_FREQS = HEAD_DIM // 4
LRU_BLOCK = 128
LRU_C = 8.0
LRU_CONV = 4
CONF_WIDTH = 31
FFN_CONV = 3
NORM_EPS = 1e-6
NEG_INF = -1e30
N_MIXERS = 3

LANES = 128
SUBLANES = 8
ROW_STEP = 16
MOD_ROWS = 8
VMEM_LIMIT = 56 * 1024 * 1024


def _cparams(sem):
    return pltpu.CompilerParams(dimension_semantics=sem, vmem_limit_bytes=VMEM_LIMIT)


def _tile(n, pref):
    t = min(n, pref)
    assert n % t == 0, (n, t)
    return t


def _silu(x):
    return x * jax.nn.sigmoid(x)


def _rms_mod(x, g, shift, scale):
    ms = jnp.mean(x * x, axis=-1, keepdims=True)
    return (x * lax.rsqrt(ms + NORM_EPS) * g) * (1.0 + scale) + shift


def _mod_slice(mod_ref, k, d):
    return mod_ref[:, k * d:(k + 1) * d]


def _ada_kernel(c_ref, w_ref, b_ref, o_ref):
    s = _silu(c_ref[...]).astype(BF16)
    o_ref[...] = jnp.dot(s, w_ref[...].astype(BF16), preferred_element_type=F32) + b_ref[...]


def _ada_call(cvec, ada_w, ada_b):
    depth, d, n = ada_w.shape
    tn = _tile(n, 1024)
    return pl.pallas_call(
        _ada_kernel,
        grid=(depth, n // tn),
        in_specs=[pl.BlockSpec((MOD_ROWS, d), lambda l, j: (0, 0)),
                  pl.BlockSpec((None, d, tn), lambda l, j: (l, 0, j)),
                  pl.BlockSpec((None, 1, tn), lambda l, j: (l, 0, j))],
        out_specs=pl.BlockSpec((None, MOD_ROWS, tn), lambda l, j: (l, 0, j)),
        out_shape=jax.ShapeDtypeStruct((depth, MOD_ROWS, n), F32),
        compiler_params=_cparams(("parallel", "parallel")),
        name="ada",
    )(cvec, ada_w, ada_b.reshape(depth, 1, n))


def _rope(y, cos, sp, sm):
    return y * cos + pltpu.roll(y, 32, 1) * sp + pltpu.roll(y, HEAD_DIM - 32, 1) * sm


def _proj_kernel(*refs, d, tn, n_out, has_bias, glu, rope_cols, n_cast):
    it = iter(refs)
    x_ref, mod_ref, g_ref, w_ref = (next(it) for _ in range(4))
    b_ref = next(it) if has_bias else None
    rope_refs = [next(it) for _ in range(3 if rope_cols else 0)]
    cast_in = [next(it) for _ in range(n_cast)]
    o_ref = next(it)
    for src_ref in cast_in:
        next(it)[...] = src_ref[...].astype(BF16)
    h = _rms_mod(x_ref[...], g_ref[...], _mod_slice(mod_ref, 0, d), _mod_slice(mod_ref, 1, d)).astype(BF16)
    if rope_cols:
        cos, sp, sm = (r[...] for r in rope_refs)
    for c0 in range(0, n_out, tn):
        y = jnp.dot(h, w_ref[:, c0:c0 + tn], preferred_element_type=F32)
        if has_bias:
            y = y + b_ref[:, c0:c0 + tn]
        if glu:
            y2 = jnp.dot(h, w_ref[:, n_out + c0:n_out + c0 + tn], preferred_element_type=F32)
            if has_bias:
                y2 = y2 + b_ref[:, n_out + c0:n_out + c0 + tn]
            y = y * jax.nn.sigmoid(y2)
        if c0 < rope_cols:
            for c in range(0, tn, HEAD_DIM):
                o_ref[:, c0 + c:c0 + c + HEAD_DIM] = _rope(y[:, c:c + HEAD_DIM], cos, sp, sm).astype(o_ref.dtype)
        else:
            o_ref[:, c0:c0 + tn] = y.astype(o_ref.dtype)


CAST_BLOCKS = 32


def _proj_call(x, mod, mod_row, norm_g, w, bias=None, *, glu=False, rope=None, rope_cols=0,
               out_dtype=F32, tm_pref=512, tn_pref=512, cast=(), name="proj"):
    nb, s, d = x.shape
    n_w = w.shape[1]
    n_out = n_w // 2 if glu else n_w
    tm, tn = _tile(s, tm_pref), _tile(n_out, tn_pref)
    mod_map = (lambda b, i: (b, 0, 0)) if mod_row is None else (lambda b, i: (mod_row, 0, 0))
    const = lambda shape: pl.BlockSpec(shape, lambda b, i: (0, 0), pipeline_mode=pl.Buffered(1))
    args = [x, mod, norm_g.reshape(1, d), w]
    specs = [pl.BlockSpec((None, tm, d), lambda b, i: (b, i, 0)),
             pl.BlockSpec((None, 1, mod.shape[-1]), mod_map),
             const((1, d)), const((d, n_w))]
    if bias is not None:
        args.append(bias.reshape(1, n_w))
        specs.append(const((1, n_w)))
    if rope is not None:
        assert rope_cols % tn == 0
        for t in rope:
            args.append(t)
            specs.append(pl.BlockSpec((tm, HEAD_DIM), lambda b, i: (i, 0)))
    n_i = s // tm
    n_steps = nb * n_i
    row_tile = 2 * SUBLANES
    ride = (bool(cast) and n_steps % CAST_BLOCKS == 0
            and all(a.shape[1] % (CAST_BLOCKS * row_tile) == 0 for a, _ in cast))
    out_specs = [pl.BlockSpec((None, tm, n_out), lambda b, i: (b, i, 0))]
    out_shape = [jax.ShapeDtypeStruct((nb, s, n_out), out_dtype)]
    if ride:
        blk_of = lambda b, i: ((b * n_i + i) * CAST_BLOCKS) // n_steps
        for a, layer in cast:
            blk = (a.shape[1] // CAST_BLOCKS, a.shape[2])
            args.append(a)
            specs.append(pl.BlockSpec((None,) + blk, lambda b, i, layer=layer: (layer, blk_of(b, i), 0)))
            out_specs.append(pl.BlockSpec(blk, lambda b, i: (blk_of(b, i), 0)))
            out_shape.append(jax.ShapeDtypeStruct(a.shape[1:], BF16))
    kern = functools.partial(_proj_kernel, d=d, tn=tn, n_out=n_out, has_bias=bias is not None, glu=glu,
                             rope_cols=rope_cols if rope is not None else 0, n_cast=len(cast) if ride else 0)
    outs = pl.pallas_call(
        kern,
        grid=(nb, n_i),
        in_specs=specs,
        out_specs=out_specs,
        out_shape=out_shape,
        compiler_params=_cparams(("arbitrary", "arbitrary") if ride else ("parallel", "parallel")),
        name=name,
    )(*args)
    return outs[0], (list(outs[1:]) if ride else [a[layer].astype(BF16) for a, layer in cast])


def _rope_tables(s):
    t = jnp.arange(s)
    row = (t // GRID_W).astype(F32)
    col = (t % GRID_W).astype(F32)
    freq = ROPE_THETA ** (-jnp.arange(ROPE_FREQS, dtype=F32) / ROPE_FREQS)
    dd = jnp.arange(HEAD_DIM)
    axis, half, f = dd // (2 * ROPE_FREQS), (dd % (2 * ROPE_FREQS)) // ROPE_FREQS, dd % ROPE_FREQS
    pos = jnp.where(axis[None, :] == 0, row[:, None], col[:, None])
    ang = pos * freq[f][None, :]
    cos, sin = jnp.cos(ang), jnp.sin(ang)
    sp = jnp.where(half[None, :] == 1, sin, 0.0)
    sm = jnp.where(half[None, :] == 0, -sin, 0.0)
    return cos, sp, sm


def _fold_lanes(parts, op):
    tiles = [p[:, j:j + LANES] for p in parts for j in range(0, p.shape[1], LANES)]
    acc = tiles[0]
    for t in tiles[1:]:
        acc = op(acc, t)
    return acc


ATT_HEADS_PER_UNIT = 4


def _attn_heads(q_ref, k_parts, v_parts, sink_ref, o_ref, masks):
    tq = q_ref.shape[0]
    hpu = ATT_HEADS_PER_UNIT
    log2e = 1.4426950408889634
    c = (HEAD_DIM ** -0.5) * log2e
    grp = lax.broadcasted_iota(jnp.int32, (hpu * tq, 1), 0) // tq
    units = [(h0 // GQA_GROUP, h0) for h0 in range(0, N_HEADS, hpu)]

    def scores(kh, h0):
        ksl = slice(kh * HEAD_DIM, (kh + 1) * HEAD_DIM)
        qg = jnp.concatenate([q_ref[:, (h0 + g) * HEAD_DIM:(h0 + g + 1) * HEAD_DIM] for g in range(hpu)], axis=0)
        out = []
        for r, mask in zip(k_parts, masks):
            s = lax.dot_general(qg, r[:, ksl], (((1,), (1,)), ((), ())), preferred_element_type=F32) * c
            out.append(s if mask is None else jnp.where(mask, s, NEG_INF))
        return out

    def finish(kh, h0, ps, l):
        ksl = slice(kh * HEAD_DIM, (kh + 1) * HEAD_DIM)
        o = None
        for p, r in zip(ps, v_parts):
            pv = jnp.dot(p.astype(BF16), r[:, ksl], preferred_element_type=F32)
            o = pv if o is None else o + pv
        o = o / l
        for g in range(hpu):
            o_ref[:, (h0 + g) * HEAD_DIM:(h0 + g + 1) * HEAD_DIM] = o[g * tq:(g + 1) * tq].astype(o_ref.dtype)

    nxt = scores(*units[0])
    pending = None
    for u, (kh, h0) in enumerate(units):
        ss = nxt
        if u + 1 < len(units):
            nxt = scores(*units[u + 1])
        if pending is not None:
            finish(*pending)
        sink = jnp.full((hpu * tq, 1), sink_ref[h0], F32)
        for g in range(1, hpu):
            sink = jnp.where(grp == g, sink_ref[h0 + g], sink)
        sink = sink * log2e
        m = jnp.maximum(jnp.max(_fold_lanes(ss, jnp.maximum), axis=-1, keepdims=True), sink)
        ps = [jnp.exp2(s - m) for s in ss]
        l = jnp.sum(_fold_lanes(ps, jnp.add), axis=-1, keepdims=True) + jnp.exp2(sink - m)
        pending = (kh, h0, ps, l)
    finish(*pending)


def _attn_kernel(sink_ref, q_ref, kp_ref, kc_ref, kn_ref, vp_ref, vc_ref, vn_ref, kx_ref, vx_ref, o_ref):
    n = pl.program_id(1)
    nblk = pl.num_programs(1)
    tq = ATT_BLOCK
    r = lax.broadcasted_iota(jnp.int32, (ATT_HEADS_PER_UNIT * tq, tq), 0) % tq
    col = lax.broadcasted_iota(jnp.int32, (ATT_HEADS_PER_UNIT * tq, tq), 1)
    mask_prev = (col >= r) & (n > 0)
    mask_next = (col <= r) & (n < nblk - 1)
    _attn_heads(q_ref, [kp_ref, kc_ref, kn_ref, kx_ref], [vp_ref, vc_ref, vn_ref, vx_ref], sink_ref, o_ref,
                [mask_prev, None, mask_next, None])


def _attn_call(qkv, qkv_ctx, sink):
    b, s, _ = qkv.shape
    n_ctx = qkv_ctx.shape[1]
    tq = ATT_BLOCK
    nblk = s // tq
    qc, kc = N_HEADS * HEAD_DIM, N_KV_HEADS * HEAD_DIM
    kblk, vblk = qc // kc, qc // kc + 1
    kv_spec = lambda off, cb: pl.BlockSpec(
        (None, tq, kc), lambda bb, n: (bb, jnp.clip(n + off, 0, nblk - 1), cb))
    return pl.pallas_call(
        _attn_kernel,
        grid=(b, nblk),
        in_specs=[pl.BlockSpec(memory_space=pltpu.SMEM),
                  pl.BlockSpec((None, tq, qc), lambda bb, n: (bb, n, 0)),
                  kv_spec(-1, kblk), kv_spec(0, kblk), kv_spec(1, kblk),
                  kv_spec(-1, vblk), kv_spec(0, vblk), kv_spec(1, vblk),
                  pl.BlockSpec((None, n_ctx, kc), lambda bb, n: (bb, 0, kblk)),
                  pl.BlockSpec((None, n_ctx, kc), lambda bb, n: (bb, 0, vblk))],
        out_specs=pl.BlockSpec((None, tq, qc), lambda bb, n: (bb, n, 0)),
        out_shape=jax.ShapeDtypeStruct((b, s, qc), BF16),
        compiler_params=_cparams(("parallel", "parallel")),
        name="attn",
    )(sink, qkv, qkv, qkv, qkv, qkv, qkv, qkv, qkv_ctx, qkv_ctx)


def _attn_ctx_kernel(sink_ref, q_ref, kx_ref, vx_ref, o_ref):
    _attn_heads(q_ref, [kx_ref], [vx_ref], sink_ref, o_ref, [None])


def _attn_ctx_call(qkv_ctx, sink):
    b, n_ctx, _ = qkv_ctx.shape
    qc, kc = N_HEADS * HEAD_DIM, N_KV_HEADS * HEAD_DIM
    tq = _tile(n_ctx, ATT_BLOCK)
    return pl.pallas_call(
        _attn_ctx_kernel,
        grid=(b, n_ctx // tq),
        in_specs=[pl.BlockSpec(memory_space=pltpu.SMEM),
                  pl.BlockSpec((None, tq, qc), lambda bb, n: (bb, n, 0)),
                  pl.BlockSpec((None, n_ctx, kc), lambda bb, n: (bb, 0, qc // kc)),
                  pl.BlockSpec((None, n_ctx, kc), lambda bb, n: (bb, 0, qc // kc + 1))],
        out_specs=pl.BlockSpec((None, tq, qc), lambda bb, n: (bb, n, 0)),
        out_shape=jax.ShapeDtypeStruct((b, n_ctx, qc), BF16),
        compiler_params=_cparams(("parallel", "parallel")),
        name="attn_ctx",
    )(sink, qkv_ctx, qkv_ctx, qkv_ctx)


def _oproj_kernel(a_ref, w_ref, x_ref, mod_ref, o_ref, *, d, gate_idx):
    y = jnp.dot(a_ref[...], w_ref[...], preferred_element_type=F32)
    o_ref[...] = x_ref[...] + _mod_slice(mod_ref, gate_idx, d) * y


def _oproj_call(a, w, x, mod, mod_row, *, gate_idx=2, tm_pref=1024, name="oproj"):
    nb, s, d = x.shape
    k = a.shape[-1]
    tm = _tile(s, tm_pref)
    mod_map = (lambda b, i: (b, 0, 0)) if mod_row is None else (lambda b, i: (mod_row, 0, 0))
    return pl.pallas_call(
        functools.partial(_oproj_kernel, d=d, gate_idx=gate_idx),
        grid=(nb, s // tm),
        in_specs=[pl.BlockSpec((None, tm, k), lambda b, i: (b, i, 0)),
                  pl.BlockSpec((k, d), lambda b, i: (0, 0), pipeline_mode=pl.Buffered(1)),
                  pl.BlockSpec((None, tm, d), lambda b, i: (b, i, 0)),
                  pl.BlockSpec((None, 1, mod.shape[-1]), mod_map)],
        out_specs=pl.BlockSpec((None, tm, d), lambda b, i: (b, i, 0)),
        out_shape=jax.ShapeDtypeStruct((nb, s, d), F32),
        compiler_params=_cparams(("parallel", "parallel")),
        name=name,
    )(a, w, x, mod)


FFN_CHUNK = 1024
FFN_SUB = 512


def _ffn_kernel(*refs, d, tm, ff, n_i, seq, final_norm):
    it = iter(refs)
    xp_ref, x_ref, xn_ref, mod_ref, g_ref, cw_ref, cb_ref = (next(it) for _ in range(7))
    fg_ref = next(it) if final_norm else None
    wup_hbm, wdn_hbm, o_ref = next(it), next(it), next(it)
    h_ref, acc_ref, wg_buf, wv_buf, wd_buf, sem = (next(it) for _ in range(6))
    t, n_t = pl.program_id(0), pl.num_programs(0)
    i = t % n_i
    halo = SUBLANES
    n_full, tail = ff // FFN_CHUNK, ff % FFN_CHUNK
    chunks = [(k * FFN_CHUNK, FFN_CHUNK) for k in range(n_full)] + ([(n_full * FFN_CHUNK, tail)] if tail else [])
    n_chunks = len(chunks)
    cross_tile = n_chunks % 2 == 0

    def copies(c0, width, slot):
        return (pltpu.make_async_copy(wup_hbm.at[:, pl.ds(c0, width)], wg_buf.at[slot, :, pl.ds(0, width)],
                                      sem.at[slot, 0]),
                pltpu.make_async_copy(wup_hbm.at[:, pl.ds(ff + c0, width)], wv_buf.at[slot, :, pl.ds(0, width)],
                                      sem.at[slot, 1]),
                pltpu.make_async_copy(wdn_hbm.at[pl.ds(c0, width), :], wd_buf.at[slot, pl.ds(0, width), :],
                                      sem.at[slot, 2]))

    def start(c0, width, slot):
        for cp in copies(c0, width, slot):
            cp.start()

    def wait(c0, width, slot):
        for cp in copies(c0, width, slot):
            cp.wait()

    if cross_tile:
        @pl.when(t == 0)
        def _():
            start(*chunks[0], 0)
    else:
        start(*chunks[0], 0)

    rows = lambda v: jnp.broadcast_to(v, (ROW_STEP, d))
    g, shift, scale = rows(g_ref[...]), rows(_mod_slice(mod_ref, 3, d)), rows(_mod_slice(mod_ref, 4, d))
    for e0 in range(0, tm + 2 * halo, ROW_STEP):
        if e0 == 0:
            xs = jnp.concatenate([xp_ref[...], x_ref[0:ROW_STEP - halo]], axis=0)
        elif e0 + ROW_STEP > tm + halo:
            xs = jnp.concatenate([x_ref[e0 - halo:tm], xn_ref[...]], axis=0)
        else:
            xs = x_ref[e0 - halo:e0 - halo + ROW_STEP]
        h_ref[e0:e0 + ROW_STEP] = _rms_mod(xs, g, shift, scale).astype(BF16)
    row = lax.broadcasted_iota(jnp.int32, (tm + 2 * halo, 1), 0)
    keep = ((row >= halo) | (i > 0)) & ((row < tm + halo) | (i < n_i - 1))
    pos = lax.broadcasted_iota(jnp.int32, (tm, 1), 0) % seq
    tap_ok = [None if seq >= tm or k == FFN_CONV // 2 else
              (pos + (k - FFN_CONV // 2) >= 0) & (pos + (k - FFN_CONV // 2) < seq) for k in range(FFN_CONV)]
    gate2 = rows(_mod_slice(mod_ref, 5, d))
    fg = rows(fg_ref[...]) if final_norm else None

    def compute(c0, width, slot, first, last):
        h = h_ref[...]
        for j in range(0, width, FFN_SUB):
            gate = jnp.dot(h, wg_buf[slot, :, j:j + FFN_SUB], preferred_element_type=F32)
            val = jnp.dot(h, wv_buf[slot, :, j:j + FFN_SUB], preferred_element_type=F32)[halo:halo + tm]
            gate = jnp.where(keep, gate, 0.0)
            cols = pl.ds(pl.multiple_of(c0 + j, FFN_SUB), FFN_SUB)
            cw = cw_ref[:, cols]
            gc = cb_ref[:, cols]
            for k in range(FFN_CONV):
                off = halo + k - FFN_CONV // 2
                gk = gate[off:off + tm]
                if tap_ok[k] is not None:
                    gk = jnp.where(tap_ok[k], gk, 0.0)
                gc = gc + cw[k:k + 1] * gk
            act = (_silu(gc) * val).astype(BF16)
            part = jnp.dot(act, wd_buf[slot, j:j + FFN_SUB, :], preferred_element_type=F32)
            if first and j == 0:
                acc_ref[...] = part
            elif last and j + FFN_SUB >= width:
                for r0 in range(0, tm, ROW_STEP):
                    out = x_ref[r0:r0 + ROW_STEP] + gate2 * (acc_ref[r0:r0 + ROW_STEP] + part[r0:r0 + ROW_STEP])
                    if final_norm:
                        ms = jnp.mean(out * out, axis=-1, keepdims=True)
                        out = out * lax.rsqrt(ms + NORM_EPS) * fg
                    o_ref[r0:r0 + ROW_STEP] = out
            else:
                acc_ref[...] += part

    def chunk(k, static):
        slot = k % 2
        if static:
            if k + 1 < n_chunks:
                start(*chunks[k + 1], 1 - slot)
            elif cross_tile:
                @pl.when(t + 1 < n_t)
                def _():
                    start(*chunks[0], 0)
            wait(*chunks[k], slot)
            compute(*chunks[k], slot, k == 0, k == n_chunks - 1)
        else:
            start((k + 1) * FFN_CHUNK, FFN_CHUNK, 1 - slot)
            wait(k * FFN_CHUNK, FFN_CHUNK, slot)
            compute(k * FFN_CHUNK, FFN_CHUNK, slot, False, False)

    head = 1
    tail_static = min(2, n_chunks - head)
    chunk(0, True)
    lax.fori_loop(head, n_chunks - tail_static, lambda k, c: (chunk(k, False), c)[1], 0)
    for k in range(n_chunks - tail_static, n_chunks):
        chunk(k, True)


def _ffn_call(x, mod, mod_row, norm_g, w_up, conv_w, conv_b, w_down, final_g=None, *, tm_pref=512, name="ffn"):
    out_shape = x.shape
    seq = x.shape[1]
    if mod_row is not None and x.shape[0] * seq <= tm_pref:
        x = x.reshape(1, x.shape[0] * seq, x.shape[2])
    nb, s, d = x.shape
    ff = w_down.shape[0]
    assert ff % FFN_SUB == 0
    tm = _tile(s, tm_pref)
    assert tm % ROW_STEP == 0 and ROW_STEP == 2 * SUBLANES and (seq >= tm or tm % seq == 0)
    assert ff > FFN_CHUNK
    n_i = s // tm
    r8 = tm // SUBLANES
    bi = lambda t: (t // n_i, t % n_i)
    mod_map = (lambda t: (t // n_i, 0, 0)) if mod_row is None else (lambda t: (mod_row, 0, 0))
    const = lambda shape: pl.BlockSpec(shape, lambda t: (0, 0), pipeline_mode=pl.Buffered(1))
    args = [x, x, x, mod, norm_g.reshape(1, d), conv_w, conv_b.reshape(1, ff)]
    specs = [pl.BlockSpec((None, SUBLANES, d), lambda t: (bi(t)[0], jnp.maximum(bi(t)[1] * r8 - 1, 0), 0)),
             pl.BlockSpec((None, tm, d), lambda t: (bi(t)[0], bi(t)[1], 0)),
             pl.BlockSpec((None, SUBLANES, d),
                          lambda t: (bi(t)[0], jnp.minimum((bi(t)[1] + 1) * r8, s // SUBLANES - 1), 0)),
             pl.BlockSpec((None, 1, mod.shape[-1]), mod_map),
             const((1, d)), const((FFN_CONV, ff)), const((1, ff))]
    if final_g is not None:
        args.append(final_g.reshape(1, d))
        specs.append(const((1, d)))
    args += [w_up, w_down]
    specs += [pl.BlockSpec(memory_space=pl.ANY), pl.BlockSpec(memory_space=pl.ANY)]
    cw = min(FFN_CHUNK, ff)
    return pl.pallas_call(
        functools.partial(_ffn_kernel, d=d, tm=tm, ff=ff, n_i=n_i, seq=seq, final_norm=final_g is not None),
        grid=(nb * n_i,),
        in_specs=specs,
        out_specs=pl.BlockSpec((None, tm, d), lambda t: (bi(t)[0], bi(t)[1], 0)),
        out_shape=jax.ShapeDtypeStruct((nb, s, d), F32),
        scratch_shapes=[pltpu.VMEM((tm + 2 * SUBLANES, d), BF16), pltpu.VMEM((tm, d), F32),
                        pltpu.VMEM((2, d, cw), BF16), pltpu.VMEM((2, d, cw), BF16), pltpu.VMEM((2, cw, d), BF16),
                        pltpu.SemaphoreType.DMA((2, 3))],
        compiler_params=_cparams(("arbitrary",)),
        name=name,
    )(*args).reshape(out_shape)


LRU_SEG = SUBLANES
LRU_CH = 64


def _lru_pitch(n):
    p = n // LRU_SEG + SUBLANES
    return p if (p // SUBLANES) % 2 == 1 else p + SUBLANES


def _fill_pitched(xp_ref, x_ref, n):
    seg, pitch = n // LRU_SEG, _lru_pitch(n)
    lanes = x_ref.shape[1]
    zero = jnp.zeros((SUBLANES, lanes), F32)
    row = lax.broadcasted_iota(jnp.int32, (SUBLANES, lanes), 0)
    xp_ref[0:SUBLANES] = zero
    for j in range(LRU_SEG):
        base = SUBLANES + j * pitch
        xp_ref[base:base + seg] = x_ref[j * seg:(j + 1) * seg]
        tail = x_ref[(j + 1) * seg - SUBLANES:(j + 1) * seg]
        head = x_ref[(j + 1) * seg:(j + 1) * seg + SUBLANES] if j + 1 < LRU_SEG else zero
        if pitch == seg + SUBLANES:
            xp_ref[base + seg:base + pitch] = jnp.where(row < SUBLANES // 2, head, tail)
        else:
            xp_ref[base + seg:base + seg + SUBLANES] = head
            xp_ref[base + pitch - SUBLANES:base + pitch] = tail


def _seg_rows(step, pitch):
    return pl.ds(SUBLANES + step, LRU_SEG, stride=pitch)


def _lru_chunk(xp_ref, hs_ref, ac_ref, pitch, s0, ch, rev, par, h, ac):
    cwb, cbb, wa, ba, wx, bx, kq = par
    lo = 0 if rev else -(LRU_CONV - 1)
    xs = [xp_ref[_seg_rows(s0 + lo + i, pitch), :] for i in range(ch + LRU_CONV - 1)]
    ucs = []
    for s in range(ch):
        u = cbb
        for k in range(LRU_CONV):
            u = u + cwb[k] * xs[s + k]
        ucs.append(u)
    uc = jnp.concatenate(ucs, axis=0)
    ucb = uc.astype(BF16)
    ta = jnp.tanh(0.5 * (jnp.dot(ucb, wa, preferred_element_type=F32) + ba))
    tx = jnp.tanh(0.5 * (jnp.dot(ucb, wx, preferred_element_type=F32) + bx))
    t = jnp.tanh(kq + kq * ta)
    q = 1.0 / (1.0 - t)
    a = (1.0 + t) * q
    bt = (jnp.sqrt(-t) * q) * ((1.0 + tx) * uc)
    hs, acs = [None] * ch, [None] * ch
    for s in (range(ch - 1, -1, -1) if rev else range(ch)):
        a_s = a[s * LRU_SEG:(s + 1) * LRU_SEG]
        h = a_s * h + bt[s * LRU_SEG:(s + 1) * LRU_SEG]
        ac = a_s * ac
        hs[s], acs[s] = h, ac
    r0 = pl.multiple_of(s0 * LRU_SEG, ch * LRU_SEG)
    hs_ref[pl.ds(r0, ch * LRU_SEG), :] = jnp.concatenate(hs, axis=0)
    ac_ref[pl.ds(r0, ch * LRU_SEG), :] = jnp.concatenate(acs, axis=0)
    return h, ac


def _lru_pass(xp_ref, scr, n, pars):
    seg, pitch = n // LRU_SEG, _lru_pitch(n)
    ch = min(LRU_CH, seg)
    nch = seg // ch
    lanes = xp_ref.shape[1]

    def body(ci, carry):
        hf, af, hr, ar = carry
        hf, af = _lru_chunk(xp_ref, scr[0], scr[1], pitch, ci * ch, ch, False, pars[0], hf, af)
        hr, ar = _lru_chunk(xp_ref, scr[2], scr[3], pitch, (nch - 1 - ci) * ch, ch, True, pars[1], hr, ar)
        return hf, af, hr, ar

    zero, one = jnp.zeros((LRU_SEG, lanes), F32), jnp.ones((LRU_SEG, lanes), F32)
    hf, af, hr, ar = lax.fori_loop(0, nch, body, (zero, one, zero, one))
    return (hf, af), (hr, ar)


def _segment_carries(h_end, a_end, h_in, rev):
    row = lax.broadcasted_iota(jnp.int32, h_end.shape, 0)
    sh = LRU_SEG - 1 if rev else 1
    ph, pa = pltpu.roll(h_end, sh, 0), pltpu.roll(a_end, sh, 0)
    c = h_in
    for j in (range(LRU_SEG - 2, -1, -1) if rev else range(1, LRU_SEG)):
        c = jnp.where(row == j, ph + pa * pltpu.roll(c, sh, 0), c)
    last = 0 if rev else LRU_SEG - 1
    end = (h_end + a_end * c)[last:last + 1]
    return c, jnp.broadcast_to(end, h_end.shape)


def _lru_finish(hs_f, ac_f, hs_r, ac_r, c_f, c_r, xp_ref, g_ref, z_ref, n):
    seg, pitch = n // LRU_SEG, _lru_pitch(n)
    ch = min(LRU_CH, seg)
    nch = seg // ch
    rows = ch * LRU_SEG
    cf = jnp.concatenate([c_f] * ch, axis=0)
    cr = jnp.concatenate([c_r] * ch, axis=0)

    def comb(ci, carry):
        r0 = pl.multiple_of(ci * rows, rows)
        sl = pl.ds(r0, rows)
        tot = (hs_f[sl, :] + ac_f[sl, :] * cf) + (hs_r[sl, :] + ac_r[sl, :] * cr)
        for s in range(ch):
            xp_ref[_seg_rows(ci * ch + s, pitch), :] = tot[s * LRU_SEG:(s + 1) * LRU_SEG]
        return carry

    lax.fori_loop(0, nch, comb, 0)
    rt = min(seg, 128)
    for j in range(LRU_SEG):
        def gate(ri, carry, j=j):
            r = pl.multiple_of(ri * rt, rt)
            tot = xp_ref[pl.ds(SUBLANES + j * pitch + r, rt), :]
            z_ref[pl.ds(j * seg + r, rt), :] = (jax.nn.gelu(g_ref[pl.ds(j * seg + r, rt), :]) * tot).astype(z_ref.dtype)
            return carry

        lax.fori_loop(0, seg // rt, gate, 0)


def _lru_kernel(gl_ref, xl_ref, gc_ref, xc_ref, cw_ref, cb_ref, wa_ref, ba_ref, wx_ref, bx_ref, lam_ref,
                zl_ref, zc_ref, xpl_ref, xpc_ref, *scr):
    s, n_ctx = xl_ref.shape[0], xc_ref.shape[0]
    lanes = xl_ref.shape[1]
    lat, ctx = scr[:4], scr[4:]
    _fill_pitched(xpc_ref, xc_ref, n_ctx)
    _fill_pitched(xpl_ref, xl_ref, s)
    pars = []
    for dr in range(2):
        nl = -lam_ref[dr]
        sp = jnp.maximum(nl, 0.0) + jnp.log1p(jnp.exp(-jnp.abs(nl)))
        cw = cw_ref[dr]
        cwb = [jnp.broadcast_to(cw[k:k + 1], (LRU_SEG, lanes)) for k in range(LRU_CONV)]
        pars.append((cwb, jnp.broadcast_to(cb_ref[dr], (LRU_SEG, lanes)), wa_ref[dr], ba_ref[dr], wx_ref[dr],
                     bx_ref[dr], (-0.25 * LRU_C) * sp))
    zero = jnp.zeros((LRU_SEG, lanes), F32)
    ends_c = _lru_pass(xpc_ref, ctx, n_ctx, pars)
    ends_l = _lru_pass(xpl_ref, lat, s, pars)
    c_ctx, c_lat = [], []
    for dr in range(2):
        c, state = _segment_carries(*ends_c[dr], zero, dr == 1)
        c_ctx.append(c)
        c_lat.append(_segment_carries(*ends_l[dr], state, dr == 1)[0])
    _lru_finish(*ctx, *c_ctx, xpc_ref, gc_ref, zc_ref, n_ctx)
    _lru_finish(*lat, *c_lat, xpl_ref, gl_ref, zl_ref, s)


def _lru_call(u, u_ctx, conv_w, conv_b, wa, ba, wx, bx, lam):
    b, s, r2 = u.shape
    r = r2 // 2
    n_ctx = u_ctx.shape[1]
    lanes = LRU_BLOCK
    nblk = r // lanes
    for n in (s, n_ctx):
        assert (n // LRU_SEG) % min(LRU_CH, n // LRU_SEG) == 0 and n % (LRU_SEG * ROW_STEP) == 0, n
    vec = lambda v: v.reshape(2, 1, r)
    col = lambda n, off: pl.BlockSpec((None, n, lanes), lambda bb, c: (bb, 0, c + off))
    par = lambda rows: pl.BlockSpec((2, rows, lanes), lambda bb, c: (0, 0, c))
    wsp = pl.BlockSpec((2, None, lanes, lanes), lambda bb, c: (0, c, 0, 0))
    pitched = lambda n: pltpu.VMEM((SUBLANES + LRU_SEG * _lru_pitch(n), lanes), F32)
    return pl.pallas_call(
        _lru_kernel,
        grid=(b, nblk),
        in_specs=[col(s, 0), col(s, nblk), col(n_ctx, 0), col(n_ctx, nblk),
                  par(LRU_CONV), par(1), wsp, par(1), wsp, par(1), par(1)],
        out_specs=[col(s, 0), col(n_ctx, 0)],
        out_shape=[jax.ShapeDtypeStruct((b, s, r), BF16), jax.ShapeDtypeStruct((b, n_ctx, r), BF16)],
        scratch_shapes=[pitched(s), pitched(n_ctx)] + [pltpu.VMEM((s, lanes), F32)] * 4
                       + [pltpu.VMEM((n_ctx, lanes), F32)] * 4,
        compiler_params=_cparams(("parallel", "parallel")),
        name="lru",
    )(u, u, u_ctx, u_ctx, conv_w, vec(conv_b), wa, vec(ba), wx, vec(bx), vec(lam))


CONF_HALO = 16
CONF_ROWS = 64
CONF_PARTS = 2


def _conf_kernel(zp_ref, z_ref, zn_ref, x_ref, mod_ref, dw_ref, db_ref, lg_ref, lb_ref, wo_ref, bo_ref,
                 o_ref, zw_ref, y_ref, *, d, tm):
    i, n_i = pl.program_id(1), pl.num_programs(1)
    n_slab = d // LANES
    zprev = jnp.where(i > 0, zp_ref[...], 0.0)
    znext = jnp.where(i < n_i - 1, zn_ref[...], 0.0)
    for c in range(n_slab):
        sl = slice(c * LANES, (c + 1) * LANES)
        zw_ref[c, 0:CONF_HALO] = zprev[:, sl]
        zw_ref[c, CONF_HALO:CONF_HALO + tm] = z_ref[:, sl]
        zw_ref[c, CONF_HALO + tm:2 * CONF_HALO + tm] = znext[:, sl]
    base = CONF_HALO - CONF_WIDTH // 2
    rows = min(CONF_ROWS, tm)
    pair = 2 * SUBLANES
    n_ld = rows + pair

    def slab(c, carry):
        l0 = pl.multiple_of(c * LANES, LANES)
        w = dw_ref[:, pl.ds(l0, LANES)]
        wb = [jnp.broadcast_to(w[k:k + 1], (SUBLANES, LANES)) for k in range(CONF_WIDTH)]
        bias = jnp.broadcast_to(db_ref[:, pl.ds(l0, LANES)], (SUBLANES, LANES))

        def block(rb, carry2):
            r0 = rb * rows
            z2 = [zw_ref[c, pl.ds(r0 + base + a, SUBLANES, stride=2), :] for a in range(n_ld)]
            for p in range(rows // pair):
                for odd in range(2):
                    parts = [None] * CONF_PARTS
                    for k in range(CONF_WIDTH):
                        term = wb[k] * z2[p * pair + odd + k]
                        q = k % CONF_PARTS
                        parts[q] = term if parts[q] is None else parts[q] + term
                    acc = bias
                    for q in range(CONF_PARTS):
                        acc = acc + parts[q]
                    y_ref[c, pl.ds(r0 + p * pair + odd, SUBLANES, stride=2), :] = acc
            return carry2

        lax.fori_loop(0, tm // rows, block, 0)
        return carry

    lax.fori_loop(0, n_slab, slab, 0)
    y = jnp.concatenate([y_ref[c] for c in range(n_slab)], axis=1)
    mu = jnp.mean(y, axis=-1, keepdims=True)
    yc = y - mu
    var = jnp.mean(yc * yc, axis=-1, keepdims=True)
    y = _silu(yc * lax.rsqrt(var + NORM_EPS) * lg_ref[...] + lb_ref[...])
    out = jnp.dot(y.astype(BF16), wo_ref[...], preferred_element_type=F32) + bo_ref[...]
    o_ref[...] = x_ref[...] + _mod_slice(mod_ref, 2, d) * out


def _conf_call(z, x, mod, mod_row, dw_w, dw_b, ln_g, ln_b, w_out, b_out, *, tm_pref=512, name="conf"):
    nb, s, d = x.shape
    tm = _tile(s, tm_pref)
    rh = tm // CONF_HALO
    row = lambda v: v.reshape(1, d)
    mod_map = (lambda b, i: (b, 0, 0)) if mod_row is None else (lambda b, i: (mod_row, 0, 0))
    const = lambda shape: pl.BlockSpec(shape, lambda b, i: (0, 0), pipeline_mode=pl.Buffered(1))
    return pl.pallas_call(
        functools.partial(_conf_kernel, d=d, tm=tm),
        grid=(nb, s // tm),
        in_specs=[pl.BlockSpec((None, CONF_HALO, d), lambda b, i: (b, jnp.maximum(i * rh - 1, 0), 0)),
                  pl.BlockSpec((None, tm, d), lambda b, i: (b, i, 0)),
                  pl.BlockSpec((None, CONF_HALO, d),
                               lambda b, i: (b, jnp.minimum((i + 1) * rh, s // CONF_HALO - 1), 0)),
                  pl.BlockSpec((None, tm, d), lambda b, i: (b, i, 0)),
                  pl.BlockSpec((None, 1, mod.shape[-1]), mod_map),
                  const((CONF_WIDTH, d)), const((1, d)), const((1, d)), const((1, d)),
                  const((d, d)), const((1, d))],
        out_specs=pl.BlockSpec((None, tm, d), lambda b, i: (b, i, 0)),
        out_shape=jax.ShapeDtypeStruct((nb, s, d), F32),
        scratch_shapes=[pltpu.VMEM((d // LANES, tm + 2 * CONF_HALO, LANES), F32),
                        pltpu.VMEM((d // LANES, tm, LANES), F32)],
        compiler_params=_cparams(("parallel", "parallel")),
        name=name,
    )(z, z, z, x, mod, dw_w, row(dw_b), row(ln_g), row(ln_b), w_out, row(b_out))


def kernel(x, c, ctx, c_ctx, ada_w, ada_b, norm_mix_g, norm_ffn_g, attn_w_qkv, attn_w_o, attn_sink, lru_w_in, lru_conv_w, lru_conv_b, lru_wa, lru_ba, lru_wx, lru_bx, lru_lambda, lru_w_out, conf_w_in, conf_b_in, conf_dw_w, conf_dw_b, conf_ln_g, conf_ln_b, conf_w_out, conf_b_out, ffn_w_up, ffn_conv_w, ffn_conv_b, ffn_w_down, final_norm_g):
    bsz, seq, d = x.shape
    depth = ada_w.shape[0]
    assert bsz + 1 <= MOD_ROWS
    ctx_row = bsz
    bf = lambda w: w.astype(BF16)

    cvec = jnp.zeros((MOD_ROWS, d), F32).at[:bsz].set(c).at[ctx_row].set(c_ctx)
    mods = _ada_call(cvec, ada_w, ada_b)
    rope = _rope_tables(seq)
    q_cols = N_HEADS * HEAD_DIM
    k_cols = N_KV_HEADS * HEAD_DIM
    xc = ctx

    for i in range(depth):
        last = i == depth - 1
        kind, j = i % N_MIXERS, i // N_MIXERS
        mod = mods[i][:, None, :]
        g_mix = norm_mix_g[i]
        ctx_used = (not last) or kind != 2
        ffn_f32 = ((ffn_w_up, i), (ffn_w_down, i))
        if kind == 0:
            w_qkv, w_o = bf(attn_w_qkv[j]), bf(attn_w_o[j])
            qkv, ffn_mats = _proj_call(x, mod, None, g_mix, w_qkv, rope=rope, rope_cols=q_cols + k_cols, out_dtype=BF16,
                                       cast=ffn_f32, name="qkv")
            qkv_c, _ = _proj_call(xc, mod, ctx_row, g_mix, w_qkv, out_dtype=BF16, name="qkv_ctx")
            o = _attn_call(qkv, qkv_c, attn_sink[j])
            x = _oproj_call(o, w_o, x, mod, None, name="attn_out")
            if not last:
                o_c = _attn_ctx_call(qkv_c, attn_sink[j])
                xc = _oproj_call(o_c, w_o, xc, mod, ctx_row, name="attn_out_ctx")
        elif kind == 1:
            w_in, w_out = bf(lru_w_in[j]), bf(lru_w_out[j])
            u, _ = _proj_call(x, mod, None, g_mix, w_in, name="lru_in")
            u_c, _ = _proj_call(xc, mod, ctx_row, g_mix, w_in, name="lru_in_ctx")
            ffn_mats = [bf(a[i]) for a, _ in ffn_f32]
            z, z_c = _lru_call(u, u_c, lru_conv_w[j], lru_conv_b[j], bf(lru_wa[j]), lru_ba[j], bf(lru_wx[j]),
                               lru_bx[j], lru_lambda[j])
            x = _oproj_call(z, w_out, x, mod, None, name="lru_out")
            if not last:
                xc = _oproj_call(z_c, w_out, xc, mod, ctx_row, name="lru_out_ctx")
        else:
            w_in, w_out = bf(conf_w_in[j]), bf(conf_w_out[j])
            tail = (conf_dw_w[j], conf_dw_b[j], conf_ln_g[j], conf_ln_b[j], w_out, conf_b_out[j])
            zz, ffn_mats = _proj_call(x, mod, None, g_mix, w_in, conf_b_in[j], glu=True, cast=ffn_f32, name="conf_in")
            x = _conf_call(zz, x, mod, None, *tail, name="conf_tail")
            if ctx_used and not last:
                zz_c, _ = _proj_call(xc, mod, ctx_row, g_mix, w_in, conf_b_in[j], glu=True, name="conf_in_ctx")
                xc = _conf_call(zz_c, xc, mod, ctx_row, *tail, name="conf_tail_ctx")
        ffn_w = (norm_ffn_g[i], ffn_mats[0], ffn_conv_w[i], ffn_conv_b[i], ffn_mats[1])
        x = _ffn_call(x, mod, None, *ffn_w, final_g=final_norm_g if last else None, name="ffn")
        if not last:
            xc = _ffn_call(xc, mod, ctx_row, *ffn_w, name="ffn_ctx")
    return x
```

```python
import functools

import jax
import jax.numpy as jnp
from jax import lax
from jax.experimental import pallas as pl
from jax.experimental.pallas import tpu as pltpu

F32 = jnp.float32
BF16 = jnp.bfloat16

HEAD_DIM = 128
N_HEADS = 16
N_KV_HEADS = 4
GQA_GROUP = N_HEADS // N_KV_HEADS
ATT_BLOCK = 128
GRID_W = 64
ROPE_THETA = 10000.0
ROPE_FREQS = HEAD_DIM // 4
LRU_BLOCK = 128
LRU_C = 8.0
LRU_CONV = 4
CONF_WIDTH = 31
FFN_CONV = 3
NORM_EPS = 1e-6
NEG_INF = -1e30
N_MIXERS = 3

LANES = 128
SUBLANES = 8
ROW_STEP = 16
MOD_ROWS = 8
VMEM_LIMIT = 56 * 1024 * 1024


def _cparams(sem):
    return pltpu.CompilerParams(dimension_semantics=sem, vmem_limit_bytes=VMEM_LIMIT)


def _tile(n, pref):
    t = min(n, pref)
    assert n % t == 0, (n, t)
    return t


def _silu(x):
    return x * jax.nn.sigmoid(x)


def _rms_mod(x, g, shift, scale):
    ms = jnp.mean(x * x, axis=-1, keepdims=True)
    return (x * lax.rsqrt(ms + NORM_EPS) * g) * (1.0 + scale) + shift


def _mod_slice(mod_ref, k, d):
    return mod_ref[:, k * d:(k + 1) * d]


def _ada_kernel(c_ref, w_ref, b_ref, o_ref):
    s = _silu(c_ref[...]).astype(BF16)
    o_ref[...] = jnp.dot(s, w_ref[...].astype(BF16), preferred_element_type=F32) + b_ref[...]


def _ada_call(cvec, ada_w, ada_b):
    depth, d, n = ada_w.shape
    tn = _tile(n, 1024)
    return pl.pallas_call(
        _ada_kernel,
        grid=(depth, n // tn),
        in_specs=[pl.BlockSpec((MOD_ROWS, d), lambda l, j: (0, 0)),
                  pl.BlockSpec((None, d, tn), lambda l, j: (l, 0, j)),
                  pl.BlockSpec((None, 1, tn), lambda l, j: (l, 0, j))],
        out_specs=pl.BlockSpec((None, MOD_ROWS, tn), lambda l, j: (l, 0, j)),
        out_shape=jax.ShapeDtypeStruct((depth, MOD_ROWS, n), F32),
        compiler_params=_cparams(("parallel", "parallel")),
        name="ada",
    )(cvec, ada_w, ada_b.reshape(depth, 1, n))


def _rope(y, cos, sp, sm):
    return y * cos + pltpu.roll(y, 32, 1) * sp + pltpu.roll(y, HEAD_DIM - 32, 1) * sm


def _proj_kernel(*refs, d, tn, n_out, has_bias, glu, rope_cols, n_cast):
    it = iter(refs)
    x_ref, mod_ref, g_ref, w_ref = (next(it) for _ in range(4))
    b_ref = next(it) if has_bias else None
    rope_refs = [next(it) for _ in range(3 if rope_cols else 0)]
    cast_in = [next(it) for _ in range(n_cast)]
    o_ref = next(it)
    for src_ref in cast_in:
        next(it)[...] = src_ref[...].astype(BF16)
    h = _rms_mod(x_ref[...], g_ref[...], _mod_slice(mod_ref, 0, d), _mod_slice(mod_ref, 1, d)).astype(BF16)
    if rope_cols:
        cos, sp, sm = (r[...] for r in rope_refs)
    for c0 in range(0, n_out, tn):
        y = jnp.dot(h, w_ref[:, c0:c0 + tn], preferred_element_type=F32)
        if has_bias:
            y = y + b_ref[:, c0:c0 + tn]
        if glu:
            y2 = jnp.dot(h, w_ref[:, n_out + c0:n_out + c0 + tn], preferred_element_type=F32)
            if has_bias:
                y2 = y2 + b_ref[:, n_out + c0:n_out + c0 + tn]
            y = y * jax.nn.sigmoid(y2)
        if c0 < rope_cols:
            for c in range(0, tn, HEAD_DIM):
                o_ref[:, c0 + c:c0 + c + HEAD_DIM] = _rope(y[:, c:c + HEAD_DIM], cos, sp, sm).astype(o_ref.dtype)
        else:
            o_ref[:, c0:c0 + tn] = y.astype(o_ref.dtype)


CAST_BLOCKS = 32


def _proj_call(x, mod, mod_row, norm_g, w, bias=None, *, glu=False, rope=None, rope_cols=0,
               out_dtype=F32, tm_pref=512, tn_pref=512, cast=(), name="proj"):
    nb, s, d = x.shape
    n_w = w.shape[1]
    n_out = n_w // 2 if glu else n_w
    tm, tn = _tile(s, tm_pref), _tile(n_out, tn_pref)
    mod_map = (lambda b, i: (b, 0, 0)) if mod_row is None else (lambda b, i: (mod_row, 0, 0))
    const = lambda shape: pl.BlockSpec(shape, lambda b, i: (0, 0), pipeline_mode=pl.Buffered(1))
    args = [x, mod, norm_g.reshape(1, d), w]
    specs = [pl.BlockSpec((None, tm, d), lambda b, i: (b, i, 0)),
             pl.BlockSpec((None, 1, mod.shape[-1]), mod_map),
             const((1, d)), const((d, n_w))]
    if bias is not None:
        args.append(bias.reshape(1, n_w))
        specs.append(const((1, n_w)))
    if rope is not None:
        assert rope_cols % tn == 0
        for t in rope:
            args.append(t)
            specs.append(pl.BlockSpec((tm, HEAD_DIM), lambda b, i: (i, 0)))
    n_i = s // tm
    n_steps = nb * n_i
    row_tile = 2 * SUBLANES
    ride = (bool(cast) and n_steps % CAST_BLOCKS == 0
            and all(a.shape[1] % (CAST_BLOCKS * row_tile) == 0 for a, _ in cast))
    out_specs = [pl.BlockSpec((None, tm, n_out), lambda b, i: (b, i, 0))]
    out_shape = [jax.ShapeDtypeStruct((nb, s, n_out), out_dtype)]
    if ride:
        blk_of = lambda b, i: ((b * n_i + i) * CAST_BLOCKS) // n_steps
        for a, layer in cast:
            blk = (a.shape[1] // CAST_BLOCKS, a.shape[2])
            args.append(a)
            specs.append(pl.BlockSpec((None,) + blk, lambda b, i, layer=layer: (layer, blk_of(b, i), 0)))
            out_specs.append(pl.BlockSpec(blk, lambda b, i: (blk_of(b, i), 0)))
            out_shape.append(jax.ShapeDtypeStruct(a.shape[1:], BF16))
    kern = functools.partial(_proj_kernel, d=d, tn=tn, n_out=n_out, has_bias=bias is not None, glu=glu,
                             rope_cols=rope_cols if rope is not None else 0, n_cast=len(cast) if ride else 0)
    outs = pl.pallas_call(
        kern,
        grid=(nb, n_i),
        in_specs=specs,
        out_specs=out_specs,
        out_shape=out_shape,
        compiler_params=_cparams(("arbitrary", "arbitrary") if ride else ("parallel", "parallel")),
        name=name,
    )(*args)
    return outs[0], (list(outs[1:]) if ride else [a[layer].astype(BF16) for a, layer in cast])


def _rope_tables(s):
    t = jnp.arange(s)
    row = (t // GRID_W).astype(F32)
    col = (t % GRID_W).astype(F32)
    freq = ROPE_THETA ** (-jnp.arange(ROPE_FREQS, dtype=F32) / ROPE_FREQS)
    dd = jnp.arange(HEAD_DIM)
    axis, half, f = dd // (2 * ROPE_FREQS), (dd % (2 * ROPE_FREQS)) // ROPE_FREQS, dd % ROPE_FREQS
    pos = jnp.where(axis[None, :] == 0, row[:, None], col[:, None])
    ang = pos * freq[f][None, :]
    cos, sin = jnp.cos(ang), jnp.sin(ang)
    sp = jnp.where(half[None, :] == 1, sin, 0.0)
    sm = jnp.where(half[None, :] == 0, -sin, 0.0)
    return cos, sp, sm


def _fold_lanes(parts, op):
    tiles = [p[:, j:j + LANES] for p in parts for j in range(0, p.shape[1], LANES)]
    acc = tiles[0]
    for t in tiles[1:]:
        acc = op(acc, t)
    return acc


ATT_HEADS_PER_UNIT = 4


def _attn_heads(q_ref, k_parts, v_parts, sink_ref, o_ref, masks):
    tq = q_ref.shape[0]
    hpu = ATT_HEADS_PER_UNIT
    log2e = 1.4426950408889634
    c = (HEAD_DIM ** -0.5) * log2e
    grp = lax.broadcasted_iota(jnp.int32, (hpu * tq, 1), 0) // tq
    units = [(h0 // GQA_GROUP, h0) for h0 in range(0, N_HEADS, hpu)]

    def scores(kh, h0):
        ksl = slice(kh * HEAD_DIM, (kh + 1) * HEAD_DIM)
        qg = jnp.concatenate([q_ref[:, (h0 + g) * HEAD_DIM:(h0 + g + 1) * HEAD_DIM] for g in range(hpu)], axis=0)
        out = []
        for r, mask in zip(k_parts, masks):
            s = lax.dot_general(qg, r[:, ksl], (((1,), (1,)), ((), ())), preferred_element_type=F32) * c
            out.append(s if mask is None else jnp.where(mask, s, NEG_INF))
        return out

    def finish(kh, h0, ps, l):
        ksl = slice(kh * HEAD_DIM, (kh + 1) * HEAD_DIM)
        o = None
        for p, r in zip(ps, v_parts):
            pv = jnp.dot(p.astype(BF16), r[:, ksl], preferred_element_type=F32)
            o = pv if o is None else o + pv
        o = o / l
        for g in range(hpu):
            o_ref[:, (h0 + g) * HEAD_DIM:(h0 + g + 1) * HEAD_DIM] = o[g * tq:(g + 1) * tq].astype(o_ref.dtype)

    nxt = scores(*units[0])
    pending = None
    for u, (kh, h0) in enumerate(units):
        ss = nxt
        if u + 1 < len(units):
            nxt = scores(*units[u + 1])
        if pending is not None:
            finish(*pending)
        sink = jnp.full((hpu * tq, 1), sink_ref[h0], F32)
        for g in range(1, hpu):
            sink = jnp.where(grp == g, sink_ref[h0 + g], sink)
        sink = sink * log2e
        m = jnp.maximum(jnp.max(_fold_lanes(ss, jnp.maximum), axis=-1, keepdims=True), sink)
        ps = [jnp.exp2(s - m) for s in ss]
        l = jnp.sum(_fold_lanes(ps, jnp.add), axis=-1, keepdims=True) + jnp.exp2(sink - m)
        pending = (kh, h0, ps, l)
    finish(*pending)


def _attn_kernel(sink_ref, q_ref, kp_ref, kc_ref, kn_ref, vp_ref, vc_ref, vn_ref, kx_ref, vx_ref, o_ref):
    n = pl.program_id(1)
    nblk = pl.num_programs(1)
    tq = ATT_BLOCK
    r = lax.broadcasted_iota(jnp.int32, (ATT_HEADS_PER_UNIT * tq, tq), 0) % tq
    col = lax.broadcasted_iota(jnp.int32, (ATT_HEADS_PER_UNIT * tq, tq), 1)
    mask_prev = (col >= r) & (n > 0)
    mask_next = (col <= r) & (n < nblk - 1)
    _attn_heads(q_ref, [kp_ref, kc_ref, kn_ref, kx_ref], [vp_ref, vc_ref, vn_ref, vx_ref], sink_ref, o_ref,
                [mask_prev, None, mask_next, None])


def _attn_call(qkv, qkv_ctx, sink):
    b, s, _ = qkv.shape
    n_ctx = qkv_ctx.shape[1]
    tq = ATT_BLOCK
    nblk = s // tq
    qc, kc = N_HEADS * HEAD_DIM, N_KV_HEADS * HEAD_DIM
    kblk, vblk = qc // kc, qc // kc + 1
    kv_spec = lambda off, cb: pl.BlockSpec(
        (None, tq, kc), lambda bb, n: (bb, jnp.clip(n + off, 0, nblk - 1), cb))
    return pl.pallas_call(
        _attn_kernel,
        grid=(b, nblk),
        in_specs=[pl.BlockSpec(memory_space=pltpu.SMEM),
                  pl.BlockSpec((None, tq, qc), lambda bb, n: (bb, n, 0)),
                  kv_spec(-1, kblk), kv_spec(0, kblk), kv_spec(1, kblk),
                  kv_spec(-1, vblk), kv_spec(0, vblk), kv_spec(1, vblk),
                  pl.BlockSpec((None, n_ctx, kc), lambda bb, n: (bb, 0, kblk)),
                  pl.BlockSpec((None, n_ctx, kc), lambda bb, n: (bb, 0, vblk))],
        out_specs=pl.BlockSpec((None, tq, qc), lambda bb, n: (bb, n, 0)),
        out_shape=jax.ShapeDtypeStruct((b, s, qc), BF16),
        compiler_params=_cparams(("parallel", "parallel")),
        name="attn",
    )(sink, qkv, qkv, qkv, qkv, qkv, qkv, qkv, qkv_ctx, qkv_ctx)


def _attn_ctx_kernel(sink_ref, q_ref, kx_ref, vx_ref, o_ref):
    _attn_heads(q_ref, [kx_ref], [vx_ref], sink_ref, o_ref, [None])


def _attn_ctx_call(qkv_ctx, sink):
    b, n_ctx, _ = qkv_ctx.shape
    qc, kc = N_HEADS * HEAD_DIM, N_KV_HEADS * HEAD_DIM
    tq = _tile(n_ctx, ATT_BLOCK)
    return pl.pallas_call(
        _attn_ctx_kernel,
        grid=(b, n_ctx // tq),
        in_specs=[pl.BlockSpec(memory_space=pltpu.SMEM),
                  pl.BlockSpec((None, tq, qc), lambda bb, n: (bb, n, 0)),
                  pl.BlockSpec((None, n_ctx, kc), lambda bb, n: (bb, 0, qc // kc)),
                  pl.BlockSpec((None, n_ctx, kc), lambda bb, n: (bb, 0, qc // kc + 1))],
        out_specs=pl.BlockSpec((None, tq, qc), lambda bb, n: (bb, n, 0)),
        out_shape=jax.ShapeDtypeStruct((b, n_ctx, qc), BF16),
        compiler_params=_cparams(("parallel", "parallel")),
        name="attn_ctx",
    )(sink, qkv_ctx, qkv_ctx, qkv_ctx)


def _oproj_kernel(a_ref, w_ref, x_ref, mod_ref, o_ref, *, d, gate_idx):
    y = jnp.dot(a_ref[...], w_ref[...], preferred_element_type=F32)
    o_ref[...] = x_ref[...] + _mod_slice(mod_ref, gate_idx, d) * y


def _oproj_call(a, w, x, mod, mod_row, *, gate_idx=2, tm_pref=1024, name="oproj"):
    nb, s, d = x.shape
    k = a.shape[-1]
    tm = _tile(s, tm_pref)
    mod_map = (lambda b, i: (b, 0, 0)) if mod_row is None else (lambda b, i: (mod_row, 0, 0))
    return pl.pallas_call(
        functools.partial(_oproj_kernel, d=d, gate_idx=gate_idx),
        grid=(nb, s // tm),
        in_specs=[pl.BlockSpec((None, tm, k), lambda b, i: (b, i, 0)),
                  pl.BlockSpec((k, d), lambda b, i: (0, 0), pipeline_mode=pl.Buffered(1)),
                  pl.BlockSpec((None, tm, d), lambda b, i: (b, i, 0)),
                  pl.BlockSpec((None, 1, mod.shape[-1]), mod_map)],
        out_specs=pl.BlockSpec((None, tm, d), lambda b, i: (b, i, 0)),
        out_shape=jax.ShapeDtypeStruct((nb, s, d), F32),
        compiler_params=_cparams(("parallel", "parallel")),
        name=name,
    )(a, w, x, mod)


FFN_CHUNK = 1024
FFN_SUB = 512


def _ffn_kernel(*refs, d, tm, ff, n_i, seq, final_norm):
    it = iter(refs)
    xp_ref, x_ref, xn_ref, mod_ref, g_ref, cw_ref, cb_ref = (next(it) for _ in range(7))
    fg_ref = next(it) if final_norm else None
    wup_hbm, wdn_hbm, o_ref = next(it), next(it), next(it)
    h_ref, acc_ref, wg_buf, wv_buf, wd_buf, sem = (next(it) for _ in range(6))
    t, n_t = pl.program_id(0), pl.num_programs(0)
    i = t % n_i
    halo = SUBLANES
    n_full, tail = ff // FFN_CHUNK, ff % FFN_CHUNK
    chunks = [(k * FFN_CHUNK, FFN_CHUNK) for k in range(n_full)] + ([(n_full * FFN_CHUNK, tail)] if tail else [])
    n_chunks = len(chunks)
    cross_tile = n_chunks % 2 == 0

    def copies(c0, width, slot):
        return (pltpu.make_async_copy(wup_hbm.at[:, pl.ds(c0, width)], wg_buf.at[slot, :, pl.ds(0, width)],
                                      sem.at[slot, 0]),
                pltpu.make_async_copy(wup_hbm.at[:, pl.ds(ff + c0, width)], wv_buf.at[slot, :, pl.ds(0, width)],
                                      sem.at[slot, 1]),
                pltpu.make_async_copy(wdn_hbm.at[pl.ds(c0, width), :], wd_buf.at[slot, pl.ds(0, width), :],
                                      sem.at[slot, 2]))

    def start(c0, width, slot):
        for cp in copies(c0, width, slot):
            cp.start()

    def wait(c0, width, slot):
        for cp in copies(c0, width, slot):
            cp.wait()

    if cross_tile:
        @pl.when(t == 0)
        def _():
            start(*chunks[0], 0)
    else:
        start(*chunks[0], 0)

    rows = lambda v: jnp.broadcast_to(v, (ROW_STEP, d))
    g, shift, scale = rows(g_ref[...]), rows(_mod_slice(mod_ref, 3, d)), rows(_mod_slice(mod_ref, 4, d))
    for e0 in range(0, tm + 2 * halo, ROW_STEP):
        if e0 == 0:
            xs = jnp.concatenate([xp_ref[...], x_ref[0:ROW_STEP - halo]], axis=0)
        elif e0 + ROW_STEP > tm + halo:
            xs = jnp.concatenate([x_ref[e0 - halo:tm], xn_ref[...]], axis=0)
        else:
            xs = x_ref[e0 - halo:e0 - halo + ROW_STEP]
        h_ref[e0:e0 + ROW_STEP] = _rms_mod(xs, g, shift, scale).astype(BF16)
    row = lax.broadcasted_iota(jnp.int32, (tm + 2 * halo, 1), 0)
    keep = ((row >= halo) | (i > 0)) & ((row < tm + halo) | (i < n_i - 1))
    pos = lax.broadcasted_iota(jnp.int32, (tm, 1), 0) % seq
    tap_ok = [None if seq >= tm or k == FFN_CONV // 2 else
              (pos + (k - FFN_CONV // 2) >= 0) & (pos + (k - FFN_CONV // 2) < seq) for k in range(FFN_CONV)]
    gate2 = rows(_mod_slice(mod_ref, 5, d))
    fg = rows(fg_ref[...]) if final_norm else None

    def compute(c0, width, slot, first, last):
        h = h_ref[...]
        for j in range(0, width, FFN_SUB):
            gate = jnp.dot(h, wg_buf[slot, :, j:j + FFN_SUB], preferred_element_type=F32)
            val = jnp.dot(h, wv_buf[slot, :, j:j + FFN_SUB], preferred_element_type=F32)[halo:halo + tm]
            gate = jnp.where(keep, gate, 0.0)
            cols = pl.ds(pl.multiple_of(c0 + j, FFN_SUB), FFN_SUB)
            cw = cw_ref[:, cols]
            gc = cb_ref[:, cols]
            for k in range(FFN_CONV):
                off = halo + k - FFN_CONV // 2
                gk = gate[off:off + tm]
                if tap_ok[k] is not None:
                    gk = jnp.where(tap_ok[k], gk, 0.0)
                gc = gc + cw[k:k + 1] * gk
            act = (_silu(gc) * val).astype(BF16)
            part = jnp.dot(act, wd_buf[slot, j:j + FFN_SUB, :], preferred_element_type=F32)
            if first and j == 0:
                acc_ref[...] = part
            elif last and j + FFN_SUB >= width:
                for r0 in range(0, tm, ROW_STEP):
                    out = x_ref[r0:r0 + ROW_STEP] + gate2 * (acc_ref[r0:r0 + ROW_STEP] + part[r0:r0 + ROW_STEP])
                    if final_norm:
                        ms = jnp.mean(out * out, axis=-1, keepdims=True)
                        out = out * lax.rsqrt(ms + NORM_EPS) * fg
                    o_ref[r0:r0 + ROW_STEP] = out
            else:
                acc_ref[...] += part

    def chunk(k, static):
        slot = k % 2
        if static:
            if k + 1 < n_chunks:
                start(*chunks[k + 1], 1 - slot)
            elif cross_tile:
                @pl.when(t + 1 < n_t)
                def _():
                    start(*chunks[0], 0)
            wait(*chunks[k], slot)
            compute(*chunks[k], slot, k == 0, k == n_chunks - 1)
        else:
            start((k + 1) * FFN_CHUNK, FFN_CHUNK, 1 - slot)
            wait(k * FFN_CHUNK, FFN_CHUNK, slot)
            compute(k * FFN_CHUNK, FFN_CHUNK, slot, False, False)

    head = 1
    tail_static = min(2, n_chunks - head)
    chunk(0, True)
    lax.fori_loop(head, n_chunks - tail_static, lambda k, c: (chunk(k, False), c)[1], 0)
    for k in range(n_chunks - tail_static, n_chunks):
        chunk(k, True)


def _ffn_call(x, mod, mod_row, norm_g, w_up, conv_w, conv_b, w_down, final_g=None, *, tm_pref=512, name="ffn"):
    out_shape = x.shape
    seq = x.shape[1]
    if mod_row is not None and x.shape[0] * seq <= tm_pref:
        x = x.reshape(1, x.shape[0] * seq, x.shape[2])
    nb, s, d = x.shape
    ff = w_down.shape[0]
    assert ff % FFN_SUB == 0
    tm = _tile(s, tm_pref)
    assert tm % ROW_STEP == 0 and ROW_STEP == 2 * SUBLANES and (seq >= tm or tm % seq == 0)
    assert ff > FFN_CHUNK
    n_i = s // tm
    r8 = tm // SUBLANES
    bi = lambda t: (t // n_i, t % n_i)
    mod_map = (lambda t: (t // n_i, 0, 0)) if mod_row is None else (lambda t: (mod_row, 0, 0))
    const = lambda shape: pl.BlockSpec(shape, lambda t: (0, 0), pipeline_mode=pl.Buffered(1))
    args = [x, x, x, mod, norm_g.reshape(1, d), conv_w, conv_b.reshape(1, ff)]
    specs = [pl.BlockSpec((None, SUBLANES, d), lambda t: (bi(t)[0], jnp.maximum(bi(t)[1] * r8 - 1, 0), 0)),
             pl.BlockSpec((None, tm, d), lambda t: (bi(t)[0], bi(t)[1], 0)),
             pl.BlockSpec((None, SUBLANES, d),
                          lambda t: (bi(t)[0], jnp.minimum((bi(t)[1] + 1) * r8, s // SUBLANES - 1), 0)),
             pl.BlockSpec((None, 1, mod.shape[-1]), mod_map),
             const((1, d)), const((FFN_CONV, ff)), const((1, ff))]
    if final_g is not None:
        args.append(final_g.reshape(1, d))
        specs.append(const((1, d)))
    args += [w_up, w_down]
    specs += [pl.BlockSpec(memory_space=pl.ANY), pl.BlockSpec(memory_space=pl.ANY)]
    cw = min(FFN_CHUNK, ff)
    return pl.pallas_call(
        functools.partial(_ffn_kernel, d=d, tm=tm, ff=ff, n_i=n_i, seq=seq, final_norm=final_g is not None),
        grid=(nb * n_i,),
        in_specs=specs,
        out_specs=pl.BlockSpec((None, tm, d), lambda t: (bi(t)[0], bi(t)[1], 0)),
        out_shape=jax.ShapeDtypeStruct((nb, s, d), F32),
        scratch_shapes=[pltpu.VMEM((tm + 2 * SUBLANES, d), BF16), pltpu.VMEM((tm, d), F32),
                        pltpu.VMEM((2, d, cw), BF16), pltpu.VMEM((2, d, cw), BF16), pltpu.VMEM((2, cw, d), BF16),
                        pltpu.SemaphoreType.DMA((2, 3))],
        compiler_params=_cparams(("arbitrary",)),
        name=name,
    )(*args).reshape(out_shape)


LRU_SEG = SUBLANES
LRU_CH = 64


def _lru_pitch(n):
    p = n // LRU_SEG + SUBLANES
    return p if (p // SUBLANES) % 2 == 1 else p + SUBLANES


def _fill_pitched(xp_ref, x_ref, n):
    seg, pitch = n // LRU_SEG, _lru_pitch(n)
    lanes = x_ref.shape[1]
    zero = jnp.zeros((SUBLANES, lanes), F32)
    row = lax.broadcasted_iota(jnp.int32, (SUBLANES, lanes), 0)
    xp_ref[0:SUBLANES] = zero
    for j in range(LRU_SEG):
        base = SUBLANES + j * pitch
        xp_ref[base:base + seg] = x_ref[j * seg:(j + 1) * seg]
        tail = x_ref[(j + 1) * seg - SUBLANES:(j + 1) * seg]
        head = x_ref[(j + 1) * seg:(j + 1) * seg + SUBLANES] if j + 1 < LRU_SEG else zero
        if pitch == seg + SUBLANES:
            xp_ref[base + seg:base + pitch] = jnp.where(row < SUBLANES // 2, head, tail)
        else:
            xp_ref[base + seg:base + seg + SUBLANES] = head
            xp_ref[base + pitch - SUBLANES:base + pitch] = tail


def _seg_rows(step, pitch):
    return pl.ds(SUBLANES + step, LRU_SEG, stride=pitch)


def _lru_chunk(xp_ref, hs_ref, ac_ref, pitch, s0, ch, rev, par, h, ac):
    cwb, cbb, wa, ba, wx, bx, kq = par
    lo = 0 if rev else -(LRU_CONV - 1)
    xs = [xp_ref[_seg_rows(s0 + lo + i, pitch), :] for i in range(ch + LRU_CONV - 1)]
    ucs = []
    for s in range(ch):
        u = cbb
        for k in range(LRU_CONV):
            u = u + cwb[k] * xs[s + k]
        ucs.append(u)
    uc = jnp.concatenate(ucs, axis=0)
    ucb = uc.astype(BF16)
    ta = jnp.tanh(0.5 * (jnp.dot(ucb, wa, preferred_element_type=F32) + ba))
    tx = jnp.tanh(0.5 * (jnp.dot(ucb, wx, preferred_element_type=F32) + bx))
    t = jnp.tanh(kq + kq * ta)
    q = 1.0 / (1.0 - t)
    a = (1.0 + t) * q
    bt = (jnp.sqrt(-t) * q) * ((1.0 + tx) * uc)
    hs, acs = [None] * ch, [None] * ch
    for s in (range(ch - 1, -1, -1) if rev else range(ch)):
        a_s = a[s * LRU_SEG:(s + 1) * LRU_SEG]
        h = a_s * h + bt[s * LRU_SEG:(s + 1) * LRU_SEG]
        ac = a_s * ac
        hs[s], acs[s] = h, ac
    r0 = pl.multiple_of(s0 * LRU_SEG, ch * LRU_SEG)
    hs_ref[pl.ds(r0, ch * LRU_SEG), :] = jnp.concatenate(hs, axis=0)
    ac_ref[pl.ds(r0, ch * LRU_SEG), :] = jnp.concatenate(acs, axis=0)
    return h, ac


def _lru_pass(xp_ref, scr, n, pars):
    seg, pitch = n // LRU_SEG, _lru_pitch(n)
    ch = min(LRU_CH, seg)
    nch = seg // ch
    lanes = xp_ref.shape[1]

    def body(ci, carry):
        hf, af, hr, ar = carry
        hf, af = _lru_chunk(xp_ref, scr[0], scr[1], pitch, ci * ch, ch, False, pars[0], hf, af)
        hr, ar = _lru_chunk(xp_ref, scr[2], scr[3], pitch, (nch - 1 - ci) * ch, ch, True, pars[1], hr, ar)
        return hf, af, hr, ar

    zero, one = jnp.zeros((LRU_SEG, lanes), F32), jnp.ones((LRU_SEG, lanes), F32)
    hf, af, hr, ar = lax.fori_loop(0, nch, body, (zero, one, zero, one))
    return (hf, af), (hr, ar)


def _segment_carries(h_end, a_end, h_in, rev):
    row = lax.broadcasted_iota(jnp.int32, h_end.shape, 0)
    sh = LRU_SEG - 1 if rev else 1
    ph, pa = pltpu.roll(h_end, sh, 0), pltpu.roll(a_end, sh, 0)
    c = h_in
    for j in (range(LRU_SEG - 2, -1, -1) if rev else range(1, LRU_SEG)):
        c = jnp.where(row == j, ph + pa * pltpu.roll(c, sh, 0), c)
    last = 0 if rev else LRU_SEG - 1
    end = (h_end + a_end * c)[last:last + 1]
    return c, jnp.broadcast_to(end, h_end.shape)


def _lru_finish(hs_f, ac_f, hs_r, ac_r, c_f, c_r, xp_ref, g_ref, z_ref, n):
    seg, pitch = n // LRU_SEG, _lru_pitch(n)
    ch = min(LRU_CH, seg)
    nch = seg // ch
    rows = ch * LRU_SEG
    cf = jnp.concatenate([c_f] * ch, axis=0)
    cr = jnp.concatenate([c_r] * ch, axis=0)

    def comb(ci, carry):
        r0 = pl.multiple_of(ci * rows, rows)
        sl = pl.ds(r0, rows)
        tot = (hs_f[sl, :] + ac_f[sl, :] * cf) + (hs_r[sl, :] + ac_r[sl, :] * cr)
        for s in range(ch):
            xp_ref[_seg_rows(ci * ch + s, pitch), :] = tot[s * LRU_SEG:(s + 1) * LRU_SEG]
        return carry

    lax.fori_loop(0, nch, comb, 0)
    rt = min(seg, 128)
    for j in range(LRU_SEG):
        def gate(ri, carry, j=j):
            r = pl.multiple_of(ri * rt, rt)
            tot = xp_ref[pl.ds(SUBLANES + j * pitch + r, rt), :]
            z_ref[pl.ds(j * seg + r, rt), :] = (jax.nn.gelu(g_ref[pl.ds(j * seg + r, rt), :]) * tot).astype(z_ref.dtype)
            return carry

        lax.fori_loop(0, seg // rt, gate, 0)


def _lru_kernel(gl_ref, xl_ref, gc_ref, xc_ref, cw_ref, cb_ref, wa_ref, ba_ref, wx_ref, bx_ref, lam_ref,
                zl_ref, zc_ref, xpl_ref, xpc_ref, *scr):
    s, n_ctx = xl_ref.shape[0], xc_ref.shape[0]
    lanes = xl_ref.shape[1]
    lat, ctx = scr[:4], scr[4:]
    _fill_pitched(xpc_ref, xc_ref, n_ctx)
    _fill_pitched(xpl_ref, xl_ref, s)
    pars = []
    for dr in range(2):
        nl = -lam_ref[dr]
        sp = jnp.maximum(nl, 0.0) + jnp.log1p(jnp.exp(-jnp.abs(nl)))
        cw = cw_ref[dr]
        cwb = [jnp.broadcast_to(cw[k:k + 1], (LRU_SEG, lanes)) for k in range(LRU_CONV)]
        pars.append((cwb, jnp.broadcast_to(cb_ref[dr], (LRU_SEG, lanes)), wa_ref[dr], ba_ref[dr], wx_ref[dr],
                     bx_ref[dr], (-0.25 * LRU_C) * sp))
    zero = jnp.zeros((LRU_SEG, lanes), F32)
    ends_c = _lru_pass(xpc_ref, ctx, n_ctx, pars)
    ends_l = _lru_pass(xpl_ref, lat, s, pars)
    c_ctx, c_lat = [], []
    for dr in range(2):
        c, state = _segment_carries(*ends_c[dr], zero, dr == 1)
        c_ctx.append(c)
        c_lat.append(_segment_carries(*ends_l[dr], state, dr == 1)[0])
    _lru_finish(*ctx, *c_ctx, xpc_ref, gc_ref, zc_ref, n_ctx)
    _lru_finish(*lat, *c_lat, xpl_ref, gl_ref, zl_ref, s)


def _lru_call(u, u_ctx, conv_w, conv_b, wa, ba, wx, bx, lam):
    b, s, r2 = u.shape
    r = r2 // 2
    n_ctx = u_ctx.shape[1]
    lanes = LRU_BLOCK
    nblk = r // lanes
    for n in (s, n_ctx):
        assert (n // LRU_SEG) % min(LRU_CH, n // LRU_SEG) == 0 and n % (LRU_SEG * ROW_STEP) == 0, n
    vec = lambda v: v.reshape(2, 1, r)
    col = lambda n, off: pl.BlockSpec((None, n, lanes), lambda bb, c: (bb, 0, c + off))
    par = lambda rows: pl.BlockSpec((2, rows, lanes), lambda bb, c: (0, 0, c))
    wsp = pl.BlockSpec((2, None, lanes, lanes), lambda bb, c: (0, c, 0, 0))
    pitched = lambda n: pltpu.VMEM((SUBLANES + LRU_SEG * _lru_pitch(n), lanes), F32)
    return pl.pallas_call(
        _lru_kernel,
        grid=(b, nblk),
        in_specs=[col(s, 0), col(s, nblk), col(n_ctx, 0), col(n_ctx, nblk),
                  par(LRU_CONV), par(1), wsp, par(1), wsp, par(1), par(1)],
        out_specs=[col(s, 0), col(n_ctx, 0)],
        out_shape=[jax.ShapeDtypeStruct((b, s, r), BF16), jax.ShapeDtypeStruct((b, n_ctx, r), BF16)],
        scratch_shapes=[pitched(s), pitched(n_ctx)] + [pltpu.VMEM((s, lanes), F32)] * 4
                       + [pltpu.VMEM((n_ctx, lanes), F32)] * 4,
        compiler_params=_cparams(("parallel", "parallel")),
        name="lru",
    )(u, u, u_ctx, u_ctx, conv_w, vec(conv_b), wa, vec(ba), wx, vec(bx), vec(lam))


CONF_HALO = 16
CONF_ROWS = 128
CONF_PARTS = 2


def _conf_kernel(zp_ref, z_ref, zn_ref, x_ref, mod_ref, dw_ref, db_ref, lg_ref, lb_ref, wo_ref, bo_ref,
                 o_ref, zw_ref, y_ref, *, d, tm):
    i, n_i = pl.program_id(1), pl.num_programs(1)
    n_slab = d // LANES
    zprev = jnp.where(i > 0, zp_ref[...], 0.0)
    znext = jnp.where(i < n_i - 1, zn_ref[...], 0.0)
    for c in range(n_slab):
        sl = slice(c * LANES, (c + 1) * LANES)
        zw_ref[c, 0:CONF_HALO] = zprev[:, sl]
        zw_ref[c, CONF_HALO:CONF_HALO + tm] = z_ref[:, sl]
        zw_ref[c, CONF_HALO + tm:2 * CONF_HALO + tm] = znext[:, sl]
    base = CONF_HALO - CONF_WIDTH // 2
    rows = min(CONF_ROWS, tm)
    pair = 2 * SUBLANES
    n_ld = rows + pair

    def slab(c, carry):
        l0 = pl.multiple_of(c * LANES, LANES)
        w = dw_ref[:, pl.ds(l0, LANES)]
        wb = [jnp.broadcast_to(w[k:k + 1], (SUBLANES, LANES)) for k in range(CONF_WIDTH)]
        bias = jnp.broadcast_to(db_ref[:, pl.ds(l0, LANES)], (SUBLANES, LANES))

        def block(rb, carry2):
            r0 = rb * rows
            z2 = [zw_ref[c, pl.ds(r0 + base + a, SUBLANES, stride=2), :] for a in range(n_ld)]
            for p in range(rows // pair):
                for odd in range(2):
                    parts = [None] * CONF_PARTS
                    for k in range(CONF_WIDTH):
                        term = wb[k] * z2[p * pair + odd + k]
                        q = k % CONF_PARTS
                        parts[q] = term if parts[q] is None else parts[q] + term
                    acc = bias
                    for q in range(CONF_PARTS):
                        acc = acc + parts[q]
                    y_ref[c, pl.ds(r0 + p * pair + odd, SUBLANES, stride=2), :] = acc
            return carry2

        lax.fori_loop(0, tm // rows, block, 0)
        return carry

    lax.fori_loop(0, n_slab, slab, 0)
    y = jnp.concatenate([y_ref[c] for c in range(n_slab)], axis=1)
    mu = jnp.mean(y, axis=-1, keepdims=True)
    yc = y - mu
    var = jnp.mean(yc * yc, axis=-1, keepdims=True)
    y = _silu(yc * lax.rsqrt(var + NORM_EPS) * lg_ref[...] + lb_ref[...])
    out = jnp.dot(y.astype(BF16), wo_ref[...], preferred_element_type=F32) + bo_ref[...]
    o_ref[...] = x_ref[...] + _mod_slice(mod_ref, 2, d) * out


def _conf_call(z, x, mod, mod_row, dw_w, dw_b, ln_g, ln_b, w_out, b_out, *, tm_pref=512, name="conf"):
    nb, s, d = x.shape
    tm = _tile(s, tm_pref)
    rh = tm // CONF_HALO
    row = lambda v: v.reshape(1, d)
    mod_map = (lambda b, i: (b, 0, 0)) if mod_row is None else (lambda b, i: (mod_row, 0, 0))
    const = lambda shape: pl.BlockSpec(shape, lambda b, i: (0, 0), pipeline_mode=pl.Buffered(1))
    return pl.pallas_call(
        functools.partial(_conf_kernel, d=d, tm=tm),
        grid=(nb, s // tm),
        in_specs=[pl.BlockSpec((None, CONF_HALO, d), lambda b, i: (b, jnp.maximum(i * rh - 1, 0), 0)),
                  pl.BlockSpec((None, tm, d), lambda b, i: (b, i, 0)),
                  pl.BlockSpec((None, CONF_HALO, d),
                               lambda b, i: (b, jnp.minimum((i + 1) * rh, s // CONF_HALO - 1), 0)),
                  pl.BlockSpec((None, tm, d), lambda b, i: (b, i, 0)),
                  pl.BlockSpec((None, 1, mod.shape[-1]), mod_map),
                  const((CONF_WIDTH, d)), const((1, d)), const((1, d)), const((1, d)),
                  const((d, d)), const((1, d))],
        out_specs=pl.BlockSpec((None, tm, d), lambda b, i: (b, i, 0)),
        out_shape=jax.ShapeDtypeStruct((nb, s, d), F32),
        scratch_shapes=[pltpu.VMEM((d // LANES, tm + 2 * CONF_HALO, LANES), F32),
                        pltpu.VMEM((d // LANES, tm, LANES), F32)],
        compiler_params=_cparams(("parallel", "parallel")),
        name=name,
    )(z, z, z, x, mod, dw_w, row(dw_b), row(ln_g), row(ln_b), w_out, row(b_out))


def kernel(x, c, ctx, c_ctx, ada_w, ada_b, norm_mix_g, norm_ffn_g, attn_w_qkv, attn_w_o, attn_sink, lru_w_in, lru_conv_w, lru_conv_b, lru_wa, lru_ba, lru_wx, lru_bx, lru_lambda, lru_w_out, conf_w_in, conf_b_in, conf_dw_w, conf_dw_b, conf_ln_g, conf_ln_b, conf_w_out, conf_b_out, ffn_w_up, ffn_conv_w, ffn_conv_b, ffn_w_down, final_norm_g):
    bsz, seq, d = x.shape
    depth = ada_w.shape[0]
    assert bsz + 1 <= MOD_ROWS
    ctx_row = bsz
    bf = lambda w: w.astype(BF16)

    cvec = jnp.zeros((MOD_ROWS, d), F32).at[:bsz].set(c).at[ctx_row].set(c_ctx)
    mods = _ada_call(cvec, ada_w, ada_b)
    rope = _rope_tables(seq)
    q_cols = N_HEADS * HEAD_DIM
    k_cols = N_KV_HEADS * HEAD_DIM
    xc = ctx

    for i in range(depth):
        last = i == depth - 1
        kind, j = i % N_MIXERS, i // N_MIXERS
        mod = mods[i][:, None, :]
        g_mix = norm_mix_g[i]
        ctx_used = (not last) or kind != 2
        ffn_f32 = ((ffn_w_up, i), (ffn_w_down, i))
        if kind == 0:
            w_qkv, w_o = bf(attn_w_qkv[j]), bf(attn_w_o[j])
            qkv, ffn_mats = _proj_call(x, mod, None, g_mix, w_qkv, rope=rope, rope_cols=q_cols + k_cols, out_dtype=BF16,
                                       cast=ffn_f32, name="qkv")
            qkv_c, _ = _proj_call(xc, mod, ctx_row, g_mix, w_qkv, out_dtype=BF16, name="qkv_ctx")
            o = _attn_call(qkv, qkv_c, attn_sink[j])
            x = _oproj_call(o, w_o, x, mod, None, name="attn_out")
            if not last:
                o_c = _attn_ctx_call(qkv_c, attn_sink[j])
                xc = _oproj_call(o_c, w_o, xc, mod, ctx_row, name="attn_out_ctx")
        elif kind == 1:
            w_in, w_out = bf(lru_w_in[j]), bf(lru_w_out[j])
            u, _ = _proj_call(x, mod, None, g_mix, w_in, name="lru_in")
            u_c, _ = _proj_call(xc, mod, ctx_row, g_mix, w_in, name="lru_in_ctx")
            ffn_mats = [bf(a[i]) for a, _ in ffn_f32]
            z, z_c = _lru_call(u, u_c, lru_conv_w[j], lru_conv_b[j], bf(lru_wa[j]), lru_ba[j], bf(lru_wx[j]),
                               lru_bx[j], lru_lambda[j])
            x = _oproj_call(z, w_out, x, mod, None, name="lru_out")
            if not last:
                xc = _oproj_call(z_c, w_out, xc, mod, ctx_row, name="lru_out_ctx")
        else:
            w_in, w_out = bf(conf_w_in[j]), bf(conf_w_out[j])
            tail = (conf_dw_w[j], conf_dw_b[j], conf_ln_g[j], conf_ln_b[j], w_out, conf_b_out[j])
            zz, ffn_mats = _proj_call(x, mod, None, g_mix, w_in, conf_b_in[j], glu=True, cast=ffn_f32, name="conf_in")
            x = _conf_call(zz, x, mod, None, *tail, name="conf_tail")
            if ctx_used and not last:
                zz_c, _ = _proj_call(xc, mod, ctx_row, g_mix, w_in, conf_b_in[j], glu=True, name="conf_in_ctx")
                xc = _conf_call(zz_c, xc, mod, ctx_row, *tail, name="conf_tail_ctx")
        ffn_w = (norm_ffn_g[i], ffn_mats[0], ffn_conv_w[i], ffn_conv_b[i], ffn_mats[1])
        x = _ffn_call(x, mod, None, *ffn_w, final_g=final_norm_g if last else None, name="ffn")
        if not last:
            xc = _ffn_call(xc, mod, ctx_row, *ffn_w, name="ffn_ctx")
    return x
```
